```python
import math
import jax, jax.numpy as jnp
from jax import lax
import numpy as np

D_MODEL = 1024
BATCH = 4
SEQ = 4096
DEPTH = 2
DEC_BATCH = 4
DEC_SEQ = 8192
PAST_LEN = 128

N_MIXERS = 2
N_MLA = (DEPTH + 1) // 2
N_SSM = DEPTH // 2
EPS = 1e-6

MLA_HEADS = 8
QK_NOPE = 128
QK_ROPE = 64
V_DIM = 128
Q_RANK = 256
KV_RANK = 256
ROPE_THETA = 10000.0
Q_BLOCK = 128
SOFTMAX_SCALE = (QK_NOPE + QK_ROPE) ** -0.5

SSM_EXPAND = 2
D_INNER = SSM_EXPAND * D_MODEL
SSM_HEADDIM = 64
SSM_HEADS = D_INNER // SSM_HEADDIM
SSM_GROUPS = 4
HEADS_PER_GROUP = SSM_HEADS // SSM_GROUPS
D_STATE = 128
CONV_K = 5
CONV_PAD = CONV_K // 2
CONV_DIM = D_INNER + 2 * SSM_GROUPS * D_STATE
CHUNK = 128
SSM_IN = D_INNER + CONV_DIM + 2 * SSM_HEADS

D_FF = 2816

kernel_name = "hybrid_mla_mamba2_macaron_encoder"


def _rmsnorm(x, g):
    x32 = x.astype(jnp.float32)
    y = x32 * lax.rsqrt(jnp.mean(x32 * x32, axis=-1, keepdims=True) + EPS)
    return (y * g.astype(jnp.float32)).astype(x.dtype)


def _swiglu(x, w_in, w_out):
    gate, up = jnp.split(x @ w_in, 2, axis=-1)
    return (jax.nn.silu(gate) * up) @ w_out


def _rope_tables(seq):
    inv = ROPE_THETA ** (-jnp.arange(0, QK_ROPE, 2, dtype=jnp.float32) / QK_ROPE)
    ang = jnp.arange(seq, dtype=jnp.float32)[:, None] * inv[None, :]
    return jnp.cos(ang), jnp.sin(ang)


def _rope(x, cos, sin):
    x32 = x.astype(jnp.float32)
    x1, x2 = jnp.split(x32, 2, axis=-1)
    return jnp.concatenate([x1 * cos - x2 * sin, x2 * cos + x1 * sin], axis=-1).astype(x.dtype)


def _mla(x, w_a, q_norm, kv_norm, w_uq, w_uk, w_uv, w_o):
    b, s, _ = x.shape
    c_q, c_kv, k_r = jnp.split(x @ w_a, [Q_RANK, Q_RANK + KV_RANK], axis=-1)
    c_q = _rmsnorm(c_q, q_norm)
    c_kv = _rmsnorm(c_kv, kv_norm)
    cos, sin = _rope_tables(s)
    q = jnp.einsum('bsr,rhe->bshe', c_q, w_uq)
    q_nope = q[..., :QK_NOPE]
    q_rope = _rope(q[..., QK_NOPE:], cos[:, None, :], sin[:, None, :])
    k_rope = _rope(k_r, cos, sin)
    q_lat = jnp.einsum('bshd,rhd->bshr', q_nope, w_uk)
    q_cat = jnp.concatenate([q_lat, q_rope], axis=-1) * SOFTMAX_SCALE
    k_cat = jnp.concatenate([c_kv, k_rope], axis=-1)
    nb = s // Q_BLOCK
    q_blocks = jnp.moveaxis(q_cat.reshape(b, nb, Q_BLOCK, MLA_HEADS, KV_RANK + QK_ROPE), 1, 0)

    def attend(qb):
        sc = jnp.einsum('bqhe,bke->bhqk', qb, k_cat).astype(jnp.float32)
        p = jax.nn.softmax(sc, axis=-1).astype(c_kv.dtype)
        return jnp.einsum('bhqk,bkr->bqhr', p, c_kv)

    o_lat = lax.map(attend, q_blocks)
    o_lat = jnp.moveaxis(o_lat, 0, 1).reshape(b, s, MLA_HEADS, KV_RANK)
    o = jnp.einsum('bshr,rhv->bshv', o_lat, w_uv).reshape(b, s, MLA_HEADS * V_DIM)
    return o @ w_o


def _ssd_scan(xs, la, bm, cm):
    b, l = xs.shape[0], xs.shape[1]
    c = l // CHUNK
    xs = xs.reshape(b, c, CHUNK, SSM_GROUPS, HEADS_PER_GROUP, SSM_HEADDIM)
    la = la.reshape(b, c, CHUNK, SSM_GROUPS, HEADS_PER_GROUP)
    bm = bm.reshape(b, c, CHUNK, SSM_GROUPS, D_STATE)
    cm = cm.reshape(b, c, CHUNK, SSM_GROUPS, D_STATE)
    acs = jnp.cumsum(la, axis=2)
    lower = jnp.tril(jnp.ones((CHUNK, CHUNK), dtype=bool))[None, None, :, :, None, None]
    seg = acs[:, :, :, None] - acs[:, :, None, :]
    decay = jnp.exp(jnp.where(lower, seg, -jnp.inf))
    cb = jnp.einsum('bclgn,bcsgn->bclsg', cm, bm)
    y_diag = jnp.einsum('bclsgr,bcsgrp->bclgrp', cb[..., None] * decay, xs)
    decay_to_end = jnp.exp(acs[:, :, -1:] - acs)
    states = jnp.einsum('bclgn,bclgr,bclgrp->bcgrpn', bm, decay_to_end, xs)
    chunk_decay = jnp.exp(acs[:, :, -1])

    def step(h, inp):
        st, dec = inp
        return h * dec[..., None, None] + st, h

    h0 = jnp.zeros_like(states[:, 0])
    _, prev = lax.scan(step, h0, (jnp.moveaxis(states, 1, 0), jnp.moveaxis(chunk_decay, 1, 0)))
    prev = jnp.moveaxis(prev, 0, 1)
    y_off = jnp.einsum('bclgn,bcgrpn,bclgr->bclgrp', cm, prev, jnp.exp(acs))
    return (y_diag + y_off).reshape(b, l, SSM_GROUPS, HEADS_PER_GROUP, SSM_HEADDIM)


def _mamba2(x, w_in, conv_w, conv_b, dt_bias, a_log, d_skip, norm_g, w_out):
    b, l, _ = x.shape
    z, xbc, dt = jnp.split(x @ w_in, [D_INNER, D_INNER + CONV_DIM], axis=-1)
    xbc = lax.conv_general_dilated(xbc, conv_w[:, None, :], (1,), [(CONV_PAD, CONV_PAD)],
                                   dimension_numbers=('NWC', 'WIO', 'NWC'),
                                   feature_group_count=CONV_DIM)
    xbc = jax.nn.silu(xbc + conv_b)
    xs, bm, cm = jnp.split(xbc, [D_INNER, D_INNER + SSM_GROUPS * D_STATE], axis=-1)
    xs = xs.reshape(b, l, SSM_GROUPS, HEADS_PER_GROUP, SSM_HEADDIM).astype(jnp.float32)
    bm = bm.reshape(b, l, SSM_GROUPS, D_STATE).astype(jnp.float32)
    cm = cm.reshape(b, l, SSM_GROUPS, D_STATE).astype(jnp.float32)
    dt = jax.nn.softplus(dt.astype(jnp.float32).reshape(b, l, 2, SSM_GROUPS, HEADS_PER_GROUP)
                         + dt_bias.astype(jnp.float32).reshape(2, SSM_GROUPS, HEADS_PER_GROUP))
    a = -jnp.exp(a_log.astype(jnp.float32)).reshape(2, SSM_GROUPS, HEADS_PER_GROUP)
    dt_f, dt_b = dt[:, :, 0], dt[:, :, 1]
    y_f = _ssd_scan(xs * dt_f[..., None], dt_f * a[0], bm, cm)
    flip = lambda t: jnp.flip(t, axis=1)
    y_b = flip(_ssd_scan(flip(xs * dt_b[..., None]), flip(dt_b * a[1]), flip(bm), flip(cm)))
    y = y_f + y_b + xs * d_skip.astype(jnp.float32).reshape(SSM_GROUPS, HEADS_PER_GROUP)[..., None]
    y = y.reshape(b, l, D_INNER) * jax.nn.silu(z.astype(jnp.float32))
    y = y.reshape(b, l, SSM_GROUPS, D_INNER // SSM_GROUPS)
    y = y * lax.rsqrt(jnp.mean(y * y, axis=-1, keepdims=True) + EPS)
    y = (y.reshape(b, l, D_INNER) * norm_g.astype(jnp.float32)).astype(x.dtype)
    return y @ w_out


def setup_inputs(seed: int = 0) -> dict:
    key = jax.random.key(seed)
    ks = jax.random.split(key, 32)
    f32 = jnp.float32

    def w(k, shape, fan_in):
        return jax.random.normal(k, shape, f32) * (fan_in ** -0.5)

    def gain(k, shape):
        return 1.0 + 0.05 * jax.random.normal(k, shape, f32)

    dt0 = jnp.exp(jax.random.uniform(ks[20], (N_SSM, 2, SSM_HEADS), f32)
                  * (math.log(0.1) - math.log(0.001)) + math.log(0.001))
    dt_bias = dt0 + jnp.log(-jnp.expm1(-dt0))
    return {
        "x_prompt": jax.random.normal(ks[0], (BATCH, SEQ, D_MODEL), f32),
        "x_sample": jax.random.normal(ks[1], (DEC_BATCH, DEC_SEQ, D_MODEL), f32),
        "ffn1_norm": gain(ks[2], (DEPTH, D_MODEL)),
        "ffn1_w_in": w(ks[3], (DEPTH, D_MODEL, 2 * D_FF), D_MODEL),
        "ffn1_w_out": w(ks[4], (DEPTH, D_FF, D_MODEL), D_FF),
        "mix_norm": gain(ks[5], (DEPTH, D_MODEL)),
        "ffn2_norm": gain(ks[6], (DEPTH, D_MODEL)),
        "ffn2_w_in": w(ks[7], (DEPTH, D_MODEL, 2 * D_FF), D_MODEL),
        "ffn2_w_out": w(ks[8], (DEPTH, D_FF, D_MODEL), D_FF),
        "mla_w_a": w(ks[9], (N_MLA, D_MODEL, Q_RANK + KV_RANK + QK_ROPE), D_MODEL),
        "mla_q_norm": gain(ks[10], (N_MLA, Q_RANK)),
        "mla_kv_norm": gain(ks[11], (N_MLA, KV_RANK)),
        "mla_w_uq": w(ks[12], (N_MLA, Q_RANK, MLA_HEADS, QK_NOPE + QK_ROPE), Q_RANK),
        "mla_w_uk": w(ks[13], (N_MLA, KV_RANK, MLA_HEADS, QK_NOPE), KV_RANK),
        "mla_w_uv": w(ks[14], (N_MLA, KV_RANK, MLA_HEADS, V_DIM), KV_RANK),
        "mla_w_o": w(ks[15], (N_MLA, MLA_HEADS * V_DIM, D_MODEL), MLA_HEADS * V_DIM),
        "ssm_w_in": w(ks[16], (N_SSM, D_MODEL, SSM_IN), D_MODEL),
        "ssm_conv_w": w(ks[17], (N_SSM, CONV_K, CONV_DIM), CONV_K),
        "ssm_conv_b": 0.02 * jax.random.normal(ks[18], (N_SSM, CONV_DIM), f32),
        "ssm_dt_bias": dt_bias,
        "ssm_a_log": jnp.log(jax.random.uniform(ks[21], (N_SSM, 2, SSM_HEADS), f32, 1.0, 16.0)),
        "ssm_d": gain(ks[22], (N_SSM, SSM_HEADS)),
        "ssm_norm": gain(ks[23], (N_SSM, D_INNER)),
        "ssm_w_out": w(ks[24], (N_SSM, D_INNER, D_MODEL), D_INNER),
        "final_norm": gain(ks[25], (D_MODEL,)),
    }


def reference(x_prompt, x_sample, ffn1_norm, ffn1_w_in, ffn1_w_out, mix_norm, ffn2_norm,
              ffn2_w_in, ffn2_w_out, mla_w_a, mla_q_norm, mla_kv_norm, mla_w_uq, mla_w_uk,
              mla_w_uv, mla_w_o, ssm_w_in, ssm_conv_w, ssm_conv_b, ssm_dt_bias, ssm_a_log,
              ssm_d, ssm_norm, ssm_w_out, final_norm):
    def trunk(x):
        for i in range(DEPTH):
            x = x + 0.5 * _swiglu(_rmsnorm(x, ffn1_norm[i]), ffn1_w_in[i], ffn1_w_out[i])
            h = _rmsnorm(x, mix_norm[i])
            j = i // N_MIXERS
            if i % N_MIXERS == 0:
                h = _mla(h, mla_w_a[j], mla_q_norm[j], mla_kv_norm[j], mla_w_uq[j],
                         mla_w_uk[j], mla_w_uv[j], mla_w_o[j])
            else:
                h = _mamba2(h, ssm_w_in[j], ssm_conv_w[j], ssm_conv_b[j], ssm_dt_bias[j],
                            ssm_a_log[j], ssm_d[j], ssm_norm[j], ssm_w_out[j])
            x = x + h
            x = x + 0.5 * _swiglu(_rmsnorm(x, ffn2_norm[i]), ffn2_w_in[i], ffn2_w_out[i])
        return _rmsnorm(x, final_norm)

    y_prompt = trunk(x_prompt)
    y_sample = trunk(x_sample)
    return (y_prompt, y_sample)
```

```python
import functools

import jax
import jax.numpy as jnp
from jax import lax
from jax.experimental import pallas as pl
from jax.experimental.pallas import tpu as pltpu

EPS = 1e-6
MLA_HEADS = 8
QK_NOPE = 128
QK_ROPE = 64
V_DIM = 128
Q_RANK = 256
KV_RANK = 256
ROPE_THETA = 10000.0
SOFTMAX_SCALE = (QK_NOPE + QK_ROPE) ** -0.5
SSM_HEADDIM = 64
SSM_HEADS = 32
SSM_GROUPS = 4
HEADS_PER_GROUP = SSM_HEADS // SSM_GROUPS
D_STATE = 128
CONV_K = 5

LANES = 128
VMEM_LIMIT_BYTES = 56 * 1024 * 1024

MXU_DTYPE = jnp.bfloat16
ROPE_PAD = LANES
QK_WIDTH = KV_RANK + ROPE_PAD
NEG_BIG = -1e30

_NT = (((1,), (1,)), ((), ()))


def _dot(a, b):
    return jnp.dot(a, b, preferred_element_type=jnp.float32)


def _dot_nt(a, b):
    return lax.dot_general(a, b, _NT, preferred_element_type=jnp.float32)


def _dot_f32(a, b, dims=None):
    if dims is None:
        return jnp.dot(a, b, preferred_element_type=jnp.float32,
                       precision=lax.Precision.HIGHEST)
    return lax.dot_general(a, b, dims, preferred_element_type=jnp.float32,
                           precision=lax.Precision.HIGHEST)


def _rms(x, g):
    return x * lax.rsqrt(jnp.mean(x * x, axis=-1, keepdims=True) + EPS) * g


def _silu(x):
    return x * jax.nn.sigmoid(x)


def _params(*sem):
    return pltpu.CompilerParams(dimension_semantics=sem, vmem_limit_bytes=VMEM_LIMIT_BYTES)


def _tile(n, want):
    t = min(n, want)
    assert n % t == 0, (n, t)
    return t


def _resident(shape):
    nd = len(shape)
    return pl.BlockSpec(shape, lambda *_: (0,) * nd, pipeline_mode=pl.Buffered(1))


def _ffn_kernel(*refs, n_chunks, chunk, final):
    if final:
        x_ref, g_ref, wg_ref, wu_ref, wo_ref, fg_ref, o_ref, a_ref = refs
    else:
        x_ref, g_ref, wg_ref, wu_ref, wo_ref, o_ref, a_ref = refs
    x = x_ref[...]
    h = _rms(x, g_ref[...]).astype(MXU_DTYPE)
    for c in range(n_chunks):
        sl = slice(c * chunk, (c + 1) * chunk)
        gate = _dot(h, wg_ref[:, sl])
        up = _dot(h, wu_ref[:, sl])
        a_ref[:, sl] = (_silu(gate) * up).astype(MXU_DTYPE)
    y = x + 0.5 * _dot(a_ref[...], wo_ref[...])
    if final:
        y = _rms(y, fg_ref[...])
    o_ref[...] = y


def _ffn(x, norm_g, w_in, w_out, final_g=None):
    t, d = x.shape
    f = w_out.shape[0]
    tm = _tile(t, 512)
    chunk = 2 * LANES if f % (2 * LANES) == 0 else f
    final = final_g is not None
    in_specs = [
        pl.BlockSpec((tm, d), lambda i: (i, 0)),
        _resident((1, d)),
        pl.BlockSpec((d, f), lambda i: (0, 0), pipeline_mode=pl.Buffered(1)),
        pl.BlockSpec((d, f), lambda i: (0, 1), pipeline_mode=pl.Buffered(1)),
        _resident((f, d)),
    ]
    args = [x, norm_g.reshape(1, d), w_in, w_in, w_out]
    if final:
        in_specs.append(_resident((1, d)))
        args.append(final_g.reshape(1, d))
    return pl.pallas_call(
        functools.partial(_ffn_kernel, n_chunks=f // chunk, chunk=chunk, final=final),
        grid=(t // tm,),
        in_specs=in_specs,
        out_specs=pl.BlockSpec((tm, d), lambda i: (i, 0)),
        out_shape=jax.ShapeDtypeStruct((t, d), jnp.float32),
        scratch_shapes=[pltpu.VMEM((tm, f), MXU_DTYPE)],
        compiler_params=_params("parallel"),
        name="ffn_final" if final else "ffn",
    )(*args)


def _mla_proj_kernel(x_ref, g_ref, wa_ref, qn_ref, kvn_ref, wqn_ref, wqa_ref, wqb_ref,
                     wukt_ref, cos_ref, sin_ref, q_ref, k_ref):
    h = _rms(x_ref[0], g_ref[...]).astype(MXU_DTYPE)
    c = _dot(h, wa_ref[...])
    c_q = _rms(c[:, :Q_RANK], qn_ref[...]).astype(MXU_DTYPE)
    c_kv = _rms(c[:, Q_RANK:Q_RANK + KV_RANK], kvn_ref[...])
    cos = cos_ref[...]
    sin = sin_ref[...]
    k0 = Q_RANK + KV_RANK
    k_rope = c[:, k0:k0 + ROPE_PAD] * cos + c[:, k0 + ROPE_PAD:k0 + 2 * ROPE_PAD] * sin
    k_ref[0, :, :KV_RANK] = c_kv.astype(MXU_DTYPE)
    k_ref[0, :, KV_RANK:] = k_rope.astype(MXU_DTYPE)
    q_nope = _dot(c_q, wqn_ref[...]).astype(MXU_DTYPE)
    q_a = _dot(c_q, wqa_ref[...])
    q_b = _dot(c_q, wqb_ref[...])
    for hd in range(MLA_HEADS):
        q_lat = _dot(q_nope[:, hd * QK_NOPE:(hd + 1) * QK_NOPE], wukt_ref[hd])
        sl = slice(hd * ROPE_PAD, (hd + 1) * ROPE_PAD)
        q_rope = q_a[:, sl] * cos + q_b[:, sl] * sin
        q_ref[0, hd, :, :KV_RANK] = (q_lat * SOFTMAX_SCALE).astype(MXU_DTYPE)
        q_ref[0, hd, :, KV_RANK:] = (q_rope * SOFTMAX_SCALE).astype(MXU_DTYPE)


def _mla_attn_kernel(q_ref, k_ref, x_ref, wuv_ref, wo_ref, o_ref, m_ref, l_ref, acc_ref,
                     *, tk, nk):
    nh, tq = q_ref.shape[1], q_ref.shape[2]
    q = q_ref[0].reshape(nh * tq, QK_WIDTH)
    m_ref[...] = jnp.full(m_ref.shape, -jnp.inf, jnp.float32)
    l_ref[...] = jnp.zeros(l_ref.shape, jnp.float32)
    acc_ref[...] = jnp.zeros(acc_ref.shape, jnp.float32)

    def body(kt, carry):
        start = pl.multiple_of(kt * tk, tk)
        k = k_ref[0, pl.ds(start, tk), :]
        s = _dot_nt(q, k)
        m_prev = m_ref[...]
        m_new = jnp.maximum(m_prev, jnp.max(s, axis=1, keepdims=True))
        p = jnp.exp(s - pltpu.repeat(m_new, tk // LANES, axis=1))
        alpha = jnp.exp(m_prev - m_new)
        l_ref[...] = alpha * l_ref[...] + jnp.sum(p, axis=1, keepdims=True)
        m_ref[...] = m_new
        acc_ref[...] = (acc_ref[...] * pltpu.repeat(alpha, KV_RANK // LANES, axis=1)
                        + _dot(p.astype(MXU_DTYPE), k[:, :KV_RANK]))
        return carry

    lax.fori_loop(0, nk, body, 0)
    o_lat = (acc_ref[...] / pltpu.repeat(l_ref[...], KV_RANK // LANES, axis=1)).astype(MXU_DTYPE)
    heads = [_dot(o_lat[hd * tq:(hd + 1) * tq], wuv_ref[hd]) for hd in range(nh)]
    o = jnp.concatenate(heads, axis=1).astype(MXU_DTYPE)
    o_ref[0] = x_ref[0] + _dot(o, wo_ref[...])


def _rope_tables(seq):
    inv = ROPE_THETA ** (-jnp.arange(0, QK_ROPE, 2, dtype=jnp.float32) / QK_ROPE)
    ang = jnp.arange(seq, dtype=jnp.float32)[:, None] * inv[None, :]
    pad = jnp.zeros((seq, ROPE_PAD - QK_ROPE), jnp.float32)
    cos, sin = jnp.cos(ang), jnp.sin(ang)
    return (jnp.concatenate([cos, cos, pad], axis=1), jnp.concatenate([sin, sin, pad], axis=1))


def _prep_mla(w_a, q_norm, kv_norm, w_uq, w_uk, w_uv, w_o, mix_norm):
    d = w_a.shape[0]
    half = QK_ROPE // 2
    k0 = Q_RANK + KV_RANK
    zpad = jnp.zeros((d, ROPE_PAD - QK_ROPE), w_a.dtype)
    wa = jnp.concatenate([w_a, zpad, -w_a[:, k0 + half:], w_a[:, k0:k0 + half], zpad], axis=1)
    r1 = w_uq[:, :, QK_NOPE:QK_NOPE + half]
    r2 = w_uq[:, :, QK_NOPE + half:]
    zq = jnp.zeros((Q_RANK, MLA_HEADS, ROPE_PAD - QK_ROPE), w_uq.dtype)
    wqa = jnp.concatenate([r1, r2, zq], axis=2).reshape(Q_RANK, MLA_HEADS * ROPE_PAD)
    wqb = jnp.concatenate([-r2, r1, zq], axis=2).reshape(Q_RANK, MLA_HEADS * ROPE_PAD)
    wqn = w_uq[:, :, :QK_NOPE].reshape(Q_RANK, MLA_HEADS * QK_NOPE)
    c = lambda w: w.astype(MXU_DTYPE)
    return dict(
        g=mix_norm.reshape(1, d), wa=c(wa), qn=q_norm.reshape(1, Q_RANK),
        kvn=kv_norm.reshape(1, KV_RANK), wqn=c(wqn), wqa=c(wqa), wqb=c(wqb),
        wukt=c(jnp.transpose(w_uk, (1, 2, 0))), wuv=c(jnp.transpose(w_uv, (1, 0, 2))),
        wo=c(w_o))


def _mla(x, p):
    b, s, d = x.shape
    cos, sin = _rope_tables(s)
    tm = _tile(s, 512)
    q, k = pl.pallas_call(
        _mla_proj_kernel,
        grid=(b, s // tm),
        in_specs=[
            pl.BlockSpec((1, tm, d), lambda bi, i: (bi, i, 0)),
            _resident((1, d)), _resident(p["wa"].shape), _resident((1, Q_RANK)),
            _resident((1, KV_RANK)), _resident(p["wqn"].shape), _resident(p["wqa"].shape),
            _resident(p["wqb"].shape), _resident(p["wukt"].shape),
            pl.BlockSpec((tm, ROPE_PAD), lambda bi, i: (i, 0)),
            pl.BlockSpec((tm, ROPE_PAD), lambda bi, i: (i, 0)),
        ],
        out_specs=[
            pl.BlockSpec((1, MLA_HEADS, tm, QK_WIDTH), lambda bi, i: (bi, 0, i, 0)),
            pl.BlockSpec((1, tm, QK_WIDTH), lambda bi, i: (bi, i, 0)),
        ],
        out_shape=[
            jax.ShapeDtypeStruct((b, MLA_HEADS, s, QK_WIDTH), MXU_DTYPE),
            jax.ShapeDtypeStruct((b, s, QK_WIDTH), MXU_DTYPE),
        ],
        compiler_params=_params("parallel", "parallel"),
        name="mla_proj",
    )(x, p["g"], p["wa"], p["qn"], p["kvn"], p["wqn"], p["wqa"], p["wqb"], p["wukt"], cos, sin)

    tq = _tile(s, 128)
    tk = _tile(s, 512)
    rows = MLA_HEADS * tq
    return pl.pallas_call(
        functools.partial(_mla_attn_kernel, tk=tk, nk=s // tk),
        grid=(b, s // tq),
        in_specs=[
            pl.BlockSpec((1, MLA_HEADS, tq, QK_WIDTH), lambda bi, i: (bi, 0, i, 0)),
            pl.BlockSpec((1, s, QK_WIDTH), lambda bi, i: (bi, 0, 0)),
            pl.BlockSpec((1, tq, d), lambda bi, i: (bi, i, 0)),
            _resident(p["wuv"].shape), _resident(p["wo"].shape),
        ],
        out_specs=pl.BlockSpec((1, tq, d), lambda bi, i: (bi, i, 0)),
        out_shape=jax.ShapeDtypeStruct((b, s, d), jnp.float32),
        scratch_shapes=[pltpu.VMEM((rows, LANES), jnp.float32),
                        pltpu.VMEM((rows, LANES), jnp.float32),
                        pltpu.VMEM((rows, KV_RANK), jnp.float32)],
        compiler_params=_params("parallel", "arbitrary"),
        name="mla_attn",
    )(q, k, x, p["wuv"], p["wo"])


def _ssm_in_kernel(x_ref, g_ref, wz_ref, wx_ref, wdt_ref, dtb_ref, z_ref, xbc_ref, dt_ref):
    h = _rms(x_ref[0], g_ref[...]).astype(MXU_DTYPE)
    z_ref[0] = _dot_nt(wz_ref[...], h)
    xbc_ref[0] = _dot_nt(wx_ref[...], h)
    dt_ref[0] = jax.nn.softplus(_dot_nt(wdt_ref[...], h) + dtb_ref[...])


def _conv_kernel(xm_ref, xp_ref, xn_ref, w_ref, b_ref, o_ref, *, nt):
    i = pl.program_id(2)
    xm = xm_ref[0]
    tl = xm.shape[1]
    xp = jnp.where(i > 0, xp_ref[0], 0.0)
    xn = jnp.where(i < nt - 1, xn_ref[0], 0.0)
    ext = jnp.concatenate([xp, xm, xn], axis=1)
    width = tl + 2 * LANES
    w = w_ref[...]
    acc = b_ref[...] + w[:, CONV_K // 2:CONV_K // 2 + 1] * xm
    for k in range(CONV_K):
        d = k - CONV_K // 2
        if d == 0:
            continue
        shifted = pltpu.roll(ext, (width - d) % width, axis=1)[:, LANES:LANES + tl]
        acc = acc + w[:, k:k + 1] * shifted
    o_ref[0] = _silu(acc)


def _ssd_chunks(xs_ref, b_ref, c_ref, dt_ref, alog_ref, y_ref, st_ref, *, rev, first):
    q = LANES
    tl = xs_ref.shape[2]
    nch = tl // q
    hp = HEADS_PER_GROUP * SSM_HEADDIM

    @pl.when(first)
    def _():
        st_ref[...] = jnp.zeros(st_ref.shape, jnp.float32)

    row = lax.broadcasted_iota(jnp.int32, (q, q), 0)
    col = lax.broadcasted_iota(jnp.int32, (q, q), 1)
    mask = (row >= col) if rev else (row <= col)
    u_sl = mask.astype(jnp.float32)
    u_ls = ((col >= row) if rev else (col <= row)).astype(jnp.float32)
    ones = jnp.ones((q, LANES), jnp.float32)
    a = -jnp.exp(alog_ref[...])

    order = range(nch - 1, -1, -1) if rev else range(nch)
    for ci in order:
        lanes = slice(ci * q, (ci + 1) * q)
        dt = dt_ref[0, :, lanes]
        la = dt * a
        cs = _dot_f32(la, u_sl)
        cs_t = _dot_f32(u_ls, la, _NT)
        tot = _dot_f32(la, ones)
        ecs = jnp.exp(cs)
        dte = jnp.exp(tot - cs)
        etot = jnp.exp(tot)
        for g in range(SSM_GROUPS):
            grows = slice(g * D_STATE, (g + 1) * D_STATE)
            bt = b_ref[0, grows, lanes]
            ct = c_ref[0, grows, lanes].astype(MXU_DTYPE)
            bct = _dot(bt.T.astype(MXU_DTYPE), ct)
            bt = bt.astype(MXU_DTYPE)
            r0 = g * hp
            prev = st_ref[r0:r0 + hp, :]
            y_off = _dot(prev.astype(MXU_DTYPE), ct)
            xdte = []
            for r in range(HEADS_PER_GROUP):
                hd = g * HEADS_PER_GROUP + r
                rows = slice(r0 + r * SSM_HEADDIM, r0 + (r + 1) * SSM_HEADDIM)
                xdt = xs_ref[0, rows, lanes] * dt[hd:hd + 1, :]
                seg = cs[hd:hd + 1, :] - cs_t[:, hd:hd + 1]
                mt = bct * jnp.exp(jnp.where(mask, seg, NEG_BIG))
                y_diag = _dot(xdt.astype(MXU_DTYPE), mt.astype(MXU_DTYPE))
                y_ref[rows, lanes] = (y_diag + y_off[r * SSM_HEADDIM:(r + 1) * SSM_HEADDIM]
                                      * ecs[hd:hd + 1, :])
                xdte.append(xdt * dte[hd:hd + 1, :])
                st_ref[rows, :] = prev[r * SSM_HEADDIM:(r + 1) * SSM_HEADDIM] * etot[hd:hd + 1, :]
            states = _dot_nt(jnp.concatenate(xdte, axis=0).astype(MXU_DTYPE), bt)
            st_ref[r0:r0 + hp, :] = st_ref[r0:r0 + hp, :] + states


def _ssd_fwd_kernel(xs_ref, b_ref, c_ref, dt_ref, alog_ref, y_ref, st_ref):
    _ssd_chunks(xs_ref, b_ref, c_ref, dt_ref, alog_ref, y_ref.at[0], st_ref,
                rev=False, first=pl.program_id(1) == 0)


def _ssd_bwd_kernel(xs_ref, b_ref, c_ref, dt_ref, alog_ref, yf_ref, z_ref, dcol_ref, ng_ref,
                    wout_ref, x_ref, o_ref, st_ref, yb_ref):
    _ssd_chunks(xs_ref, b_ref, c_ref, dt_ref, alog_ref, yb_ref, st_ref,
                rev=True, first=pl.program_id(1) == 0)
    y = yf_ref[0] + yb_ref[...] + xs_ref[0] * dcol_ref[...]
    y = y * _silu(z_ref[0])
    gsz = y.shape[0] // SSM_GROUPS
    parts = []
    for g in range(SSM_GROUPS):
        yg = y[g * gsz:(g + 1) * gsz]
        parts.append(yg * lax.rsqrt(jnp.mean(yg * yg, axis=0, keepdims=True) + EPS))
    y = (jnp.concatenate(parts, axis=0) * ng_ref[...]).astype(MXU_DTYPE)
    out_t = _dot(wout_ref[...], y)
    o_ref[0] = x_ref[0] + out_t.T


def _prep_ssm(w_in, conv_w, conv_b, dt_bias, a_log, d_skip, norm_g, w_out, mix_norm):
    d = w_in.shape[0]
    d_inner = SSM_HEADS * SSM_HEADDIM
    conv_dim = d_inner + 2 * SSM_GROUPS * D_STATE
    w_t = w_in.T.astype(MXU_DTYPE)
    return dict(
        g=mix_norm.reshape(1, d), wz=w_t[:d_inner], wx=w_t[d_inner:d_inner + conv_dim],
        wdt=w_t[d_inner + conv_dim:], dtb=dt_bias.reshape(2 * SSM_HEADS, 1),
        conv_w=conv_w.T, conv_b=conv_b.reshape(conv_dim, 1),
        alog=a_log.reshape(2, SSM_HEADS, 1),
        dcol=jnp.repeat(d_skip, SSM_HEADDIM).reshape(d_inner, 1),
        ng=norm_g.reshape(d_inner, 1), wout=w_out.T.astype(MXU_DTYPE))


def _mamba(x, p):
    b, l, d = x.shape
    d_inner = SSM_HEADS * SSM_HEADDIM
    gn = SSM_GROUPS * D_STATE
    conv_dim = d_inner + 2 * gn
    nh2 = 2 * SSM_HEADS

    tl = _tile(l, 256)
    z_t, xbc_t, dt_t = pl.pallas_call(
        _ssm_in_kernel,
        grid=(b, l // tl),
        in_specs=[
            pl.BlockSpec((1, tl, d), lambda bi, i: (bi, i, 0)),
            _resident((1, d)), _resident((d_inner, d)), _resident((conv_dim, d)),
            _resident((nh2, d)), _resident((nh2, 1)),
        ],
        out_specs=[
            pl.BlockSpec((1, d_inner, tl), lambda bi, i: (bi, 0, i)),
            pl.BlockSpec((1, conv_dim, tl), lambda bi, i: (bi, 0, i)),
            pl.BlockSpec((1, nh2, tl), lambda bi, i: (bi, 0, i)),
        ],
        out_shape=[
            jax.ShapeDtypeStruct((b, d_inner, l), jnp.float32),
            jax.ShapeDtypeStruct((b, conv_dim, l), jnp.float32),
            jax.ShapeDtypeStruct((b, nh2, l), jnp.float32),
        ],
        compiler_params=_params("parallel", "parallel"),
        name="ssm_in",
    )(x, p["g"], p["wz"], p["wx"], p["wdt"], p["dtb"])

    ct = _tile(conv_dim, 512)
    tc = _tile(l, 1024)
    nt = l // tc
    per = tc // LANES
    last_halo = l // LANES - 1
    xbc_c = pl.pallas_call(
        functools.partial(_conv_kernel, nt=nt),
        grid=(b, conv_dim // ct, nt),
        in_specs=[
            pl.BlockSpec((1, ct, tc), lambda bi, ci, i: (bi, ci, i)),
            pl.BlockSpec((1, ct, LANES), lambda bi, ci, i: (bi, ci, jnp.maximum(i * per - 1, 0))),
            pl.BlockSpec((1, ct, LANES),
                         lambda bi, ci, i: (bi, ci, jnp.minimum((i + 1) * per, last_halo))),
            pl.BlockSpec((ct, CONV_K), lambda bi, ci, i: (ci, 0)),
            pl.BlockSpec((ct, 1), lambda bi, ci, i: (ci, 0)),
        ],
        out_specs=pl.BlockSpec((1, ct, tc), lambda bi, ci, i: (bi, ci, i)),
        out_shape=jax.ShapeDtypeStruct((b, conv_dim, l), jnp.float32),
        compiler_params=_params("parallel", "parallel", "parallel"),
        name="ssm_conv",
    )(xbc_t, xbc_t, xbc_t, p["conv_w"], p["conv_b"])

    ts = _tile(l, 256)
    nb = l // ts
    b_blk = d_inner // gn
    st_shape = pltpu.VMEM((d_inner, D_STATE), jnp.float32)

    def scan_specs(tmap):
        return [
            pl.BlockSpec((1, d_inner, ts), lambda bi, i: (bi, 0, tmap(i))),
            pl.BlockSpec((1, gn, ts), lambda bi, i: (bi, b_blk, tmap(i))),
            pl.BlockSpec((1, gn, ts), lambda bi, i: (bi, b_blk + 1, tmap(i))),
        ]

    y_f = pl.pallas_call(
        _ssd_fwd_kernel,
        grid=(b, nb),
        in_specs=scan_specs(lambda i: i) + [
            pl.BlockSpec((1, SSM_HEADS, ts), lambda bi, i: (bi, 0, i)),
            pl.BlockSpec((None, SSM_HEADS, 1), lambda bi, i: (0, 0, 0)),
        ],
        out_specs=pl.BlockSpec((1, d_inner, ts), lambda bi, i: (bi, 0, i)),
        out_shape=jax.ShapeDtypeStruct((b, d_inner, l), jnp.float32),
        scratch_shapes=[st_shape],
        compiler_params=_params("parallel", "arbitrary"),
        name="ssd_fwd",
    )(xbc_c, xbc_c, xbc_c, dt_t, p["alog"])

    rmap = lambda i: nb - 1 - i
    return pl.pallas_call(
        _ssd_bwd_kernel,
        grid=(b, nb),
        in_specs=scan_specs(rmap) + [
            pl.BlockSpec((1, SSM_HEADS, ts), lambda bi, i: (bi, 1, rmap(i))),
            pl.BlockSpec((None, SSM_HEADS, 1), lambda bi, i: (1, 0, 0)),
            pl.BlockSpec((1, d_inner, ts), lambda bi, i: (bi, 0, rmap(i))),
            pl.BlockSpec((1, d_inner, ts), lambda bi, i: (bi, 0, rmap(i))),
            _resident((d_inner, 1)), _resident((d_inner, 1)), _resident((d, d_inner)),
            pl.BlockSpec((1, ts, d), lambda bi, i: (bi, rmap(i), 0)),
        ],
        out_specs=pl.BlockSpec((1, ts, d), lambda bi, i: (bi, rmap(i), 0)),
        out_shape=jax.ShapeDtypeStruct((b, l, d), jnp.float32),
        scratch_shapes=[st_shape, pltpu.VMEM((d_inner, ts), jnp.float32)],
        compiler_params=_params("parallel", "arbitrary"),
        name="ssd_bwd",
    )(xbc_c, xbc_c, xbc_c, dt_t, p["alog"], y_f, z_t, p["dcol"], p["ng"], p["wout"], x)


def kernel(x_prompt, x_sample, ffn1_norm, ffn1_w_in, ffn1_w_out, mix_norm, ffn2_norm, ffn2_w_in, ffn2_w_out, mla_w_a, mla_q_norm, mla_kv_norm, mla_w_uq, mla_w_uk, mla_w_uv, mla_w_o, ssm_w_in, ssm_conv_w, ssm_conv_b, ssm_dt_bias, ssm_a_log, ssm_d, ssm_norm, ssm_w_out, final_norm):
    depth = ffn1_norm.shape[0]
    n_mixers = 2
    c = lambda w: w.astype(MXU_DTYPE)
    w1_in, w1_out, w2_in, w2_out = c(ffn1_w_in), c(ffn1_w_out), c(ffn2_w_in), c(ffn2_w_out)
    mixers = []
    for i in range(depth):
        j = i // n_mixers
        if i % n_mixers == 0:
            mixers.append(_prep_mla(mla_w_a[j], mla_q_norm[j], mla_kv_norm[j], mla_w_uq[j],
                                    mla_w_uk[j], mla_w_uv[j], mla_w_o[j], mix_norm[i]))
        else:
            mixers.append(_prep_ssm(ssm_w_in[j], ssm_conv_w[j], ssm_conv_b[j], ssm_dt_bias[j],
                                    ssm_a_log[j], ssm_d[j], ssm_norm[j], ssm_w_out[j],
                                    mix_norm[i]))

    def trunk(x):
        b, s, d = x.shape
        for i in range(depth):
            x = _ffn(x.reshape(b * s, d), ffn1_norm[i], w1_in[i], w1_out[i]).reshape(b, s, d)
            x = _mla(x, mixers[i]) if i % n_mixers == 0 else _mamba(x, mixers[i])
            fg = final_norm if i == depth - 1 else None
            x = _ffn(x.reshape(b * s, d), ffn2_norm[i], w2_in[i], w2_out[i], fg).reshape(b, s, d)
        return x

    return (trunk(x_prompt), trunk(x_sample))
```

```python
import functools

import jax
import jax.numpy as jnp
from jax import lax
from jax.experimental import pallas as pl
from jax.experimental.pallas import tpu as pltpu

EPS = 1e-6
MLA_HEADS = 8
QK_NOPE = 128
QK_ROPE = 64
V_DIM = 128
Q_RANK = 256
KV_RANK = 256
ROPE_THETA = 10000.0
SOFTMAX_SCALE = (QK_NOPE + QK_ROPE) ** -0.5
QK_SCALE = SOFTMAX_SCALE * 1.4426950408889634
SSM_HEADDIM = 64
SSM_HEADS = 32
SSM_GROUPS = 4
HEADS_PER_GROUP = SSM_HEADS // SSM_GROUPS
D_STATE = 128
CONV_K = 5

LANES = 128
VMEM_LIMIT_BYTES = 56 * 1024 * 1024

MXU_DTYPE = jnp.bfloat16
ROPE_PAD = LANES
QK_WIDTH = KV_RANK + ROPE_PAD
NEG_BIG = -1e30

_NT = (((1,), (1,)), ((), ()))


def _dot(a, b):
    return jnp.dot(a, b, preferred_element_type=jnp.float32)


def _dot_nt(a, b):
    return lax.dot_general(a, b, _NT, preferred_element_type=jnp.float32)


def _dot_f32(a, b, dims=None):
    if dims is None:
        return jnp.dot(a, b, preferred_element_type=jnp.float32,
                       precision=lax.Precision.HIGHEST)
    return lax.dot_general(a, b, dims, preferred_element_type=jnp.float32,
                           precision=lax.Precision.HIGHEST)


def _rms(x, g):
    return x * lax.rsqrt(jnp.mean(x * x, axis=-1, keepdims=True) + EPS) * g


def _silu(x):
    return x * jax.nn.sigmoid(x)


def _params(*sem):
    return pltpu.CompilerParams(dimension_semantics=sem, vmem_limit_bytes=VMEM_LIMIT_BYTES)


def _tile(n, want):
    t = min(n, want)
    assert n % t == 0, (n, t)
    return t


def _resident(shape):
    nd = len(shape)
    return pl.BlockSpec(shape, lambda *_: (0,) * nd, pipeline_mode=pl.Buffered(1))


def _ffn_kernel(*refs, n_chunks, chunk, final):
    if final:
        x_ref, g_ref, wg_ref, wu_ref, wo_ref, fg_ref, o_ref, a_ref = refs
    else:
        x_ref, g_ref, wg_ref, wu_ref, wo_ref, o_ref, a_ref = refs
    x = x_ref[...]
    h = _rms(x, g_ref[...]).astype(MXU_DTYPE)
    for c in range(n_chunks):
        sl = slice(c * chunk, (c + 1) * chunk)
        gate = _dot(h, wg_ref[:, sl])
        up = _dot(h, wu_ref[:, sl])
        a_ref[:, sl] = (_silu(gate) * up).astype(MXU_DTYPE)
    y = x + 0.5 * _dot(a_ref[...], wo_ref[...])
    if final:
        y = _rms(y, fg_ref[...])
    o_ref[...] = y


def _ffn(x, norm_g, w_in, w_out, final_g=None):
    t, d = x.shape
    f = w_out.shape[0]
    tm = _tile(t, 512)
    chunk = 2 * LANES if f % (2 * LANES) == 0 else f
    final = final_g is not None
    in_specs = [
        pl.BlockSpec((tm, d), lambda i: (i, 0)),
        _resident((1, d)),
        pl.BlockSpec((d, f), lambda i: (0, 0), pipeline_mode=pl.Buffered(1)),
        pl.BlockSpec((d, f), lambda i: (0, 1), pipeline_mode=pl.Buffered(1)),
        _resident((f, d)),
    ]
    args = [x, norm_g.reshape(1, d), w_in, w_in, w_out]
    if final:
        in_specs.append(_resident((1, d)))
        args.append(final_g.reshape(1, d))
    return pl.pallas_call(
        functools.partial(_ffn_kernel, n_chunks=f // chunk, chunk=chunk, final=final),
        grid=(t // tm,),
        in_specs=in_specs,
        out_specs=pl.BlockSpec((tm, d), lambda i: (i, 0)),
        out_shape=jax.ShapeDtypeStruct((t, d), jnp.float32),
        scratch_shapes=[pltpu.VMEM((tm, f), MXU_DTYPE)],
        compiler_params=_params("parallel"),
        name="ffn_final" if final else "ffn",
    )(*args)


def _mla_proj_kernel(x_ref, g_ref, wa_ref, qn_ref, kvn_ref, wqn_ref, wqa_ref, wqb_ref,
                     wukt_ref, cos_ref, sin_ref, q_ref, k_ref):
    h = _rms(x_ref[0], g_ref[...]).astype(MXU_DTYPE)
    c = _dot(h, wa_ref[...])
    c_q = _rms(c[:, :Q_RANK], qn_ref[...]).astype(MXU_DTYPE)
    c_kv = _rms(c[:, Q_RANK:Q_RANK + KV_RANK], kvn_ref[...])
    cos = cos_ref[...]
    sin = sin_ref[...]
    k0 = Q_RANK + KV_RANK
    k_rope = c[:, k0:k0 + ROPE_PAD] * cos + c[:, k0 + ROPE_PAD:k0 + 2 * ROPE_PAD] * sin
    k_ref[0, :, :KV_RANK] = c_kv.astype(MXU_DTYPE)
    k_ref[0, :, KV_RANK:] = k_rope.astype(MXU_DTYPE)
    q_nope = _dot(c_q, wqn_ref[...]).astype(MXU_DTYPE)
    q_a = _dot(c_q, wqa_ref[...])
    q_b = _dot(c_q, wqb_ref[...])
    for hd in range(MLA_HEADS):
        q_lat = _dot(q_nope[:, hd * QK_NOPE:(hd + 1) * QK_NOPE], wukt_ref[hd])
        sl = slice(hd * ROPE_PAD, (hd + 1) * ROPE_PAD)
        q_rope = q_a[:, sl] * cos + q_b[:, sl] * sin
        q_ref[0, hd, :, :KV_RANK] = (q_lat * QK_SCALE).astype(MXU_DTYPE)
        q_ref[0, hd, :, KV_RANK:] = (q_rope * QK_SCALE).astype(MXU_DTYPE)


def _mla_attn_kernel(q_ref, k_ref, x_ref, wuv_ref, wo_ref, o_ref, m_ref, l_ref, acc_ref,
                     s_ref, *, tk, nk):
    nh, tq = q_ref.shape[1], q_ref.shape[2]
    q = q_ref[0].reshape(nh * tq, QK_WIDTH)
    m_ref[...] = jnp.full(m_ref.shape, -jnp.inf, jnp.float32)
    l_ref[...] = jnp.zeros(l_ref.shape, jnp.float32)
    acc_ref[...] = jnp.zeros(acc_ref.shape, jnp.float32)
    ntile = tk // LANES

    def scores(kt):
        return _dot_nt(q, k_ref[0, pl.ds(pl.multiple_of(kt * tk, tk), tk), :])

    def consume(slot, kt):
        s = s_ref[slot]
        tiles = [s[:, j * LANES:(j + 1) * LANES] for j in range(ntile)]
        m_prev = m_ref[...]
        m_new = jnp.maximum(m_prev, jnp.max(functools.reduce(jnp.maximum, tiles),
                                            axis=1, keepdims=True))
        p = [jnp.exp2(t - m_new) for t in tiles]
        alpha = jnp.exp2(m_prev - m_new)
        l_ref[...] = alpha * l_ref[...] + functools.reduce(jnp.add, p)
        m_ref[...] = m_new
        v = k_ref[0, pl.ds(pl.multiple_of(kt * tk, tk), tk), :KV_RANK]
        pv = _dot(jnp.concatenate(p, axis=1).astype(MXU_DTYPE), v)
        acc_ref[...] = acc_ref[...] * pltpu.repeat(alpha, KV_RANK // LANES, axis=1) + pv

    s_ref[0] = scores(0)

    def pair(j, carry):
        t0 = 2 * j
        s_ref[1] = scores(t0 + 1)
        consume(0, t0)
        s_ref[0] = scores(t0 + 2)
        consume(1, t0 + 1)
        return carry

    lax.fori_loop(0, nk // 2 - 1, pair, 0)
    s_ref[1] = scores(nk - 1)
    consume(0, nk - 2)
    consume(1, nk - 1)
    l = jnp.sum(l_ref[...], axis=1, keepdims=True)
    o_lat = (acc_ref[...] / l).astype(MXU_DTYPE)
    heads = [_dot(o_lat[hd * tq:(hd + 1) * tq], wuv_ref[hd]) for hd in range(nh)]
    o = jnp.concatenate(heads, axis=1).astype(MXU_DTYPE)
    o_ref[0] = x_ref[0] + _dot(o, wo_ref[...])


def _rope_tables(seq):
    inv = ROPE_THETA ** (-jnp.arange(0, QK_ROPE, 2, dtype=jnp.float32) / QK_ROPE)
    ang = jnp.arange(seq, dtype=jnp.float32)[:, None] * inv[None, :]
    pad = jnp.zeros((seq, ROPE_PAD - QK_ROPE), jnp.float32)
    cos, sin = jnp.cos(ang), jnp.sin(ang)
    return (jnp.concatenate([cos, cos, pad], axis=1), jnp.concatenate([sin, sin, pad], axis=1))


def _prep_mla(w_a, q_norm, kv_norm, w_uq, w_uk, w_uv, w_o, mix_norm):
    d = w_a.shape[0]
    half = QK_ROPE // 2
    k0 = Q_RANK + KV_RANK
    zpad = jnp.zeros((d, ROPE_PAD - QK_ROPE), w_a.dtype)
    wa = jnp.concatenate([w_a, zpad, -w_a[:, k0 + half:], w_a[:, k0:k0 + half], zpad], axis=1)
    r1 = w_uq[:, :, QK_NOPE:QK_NOPE + half]
    r2 = w_uq[:, :, QK_NOPE + half:]
    zq = jnp.zeros((Q_RANK, MLA_HEADS, ROPE_PAD - QK_ROPE), w_uq.dtype)
    wqa = jnp.concatenate([r1, r2, zq], axis=2).reshape(Q_RANK, MLA_HEADS * ROPE_PAD)
    wqb = jnp.concatenate([-r2, r1, zq], axis=2).reshape(Q_RANK, MLA_HEADS * ROPE_PAD)
    wqn = w_uq[:, :, :QK_NOPE].reshape(Q_RANK, MLA_HEADS * QK_NOPE)
    c = lambda w: w.astype(MXU_DTYPE)
    return dict(
        g=mix_norm.reshape(1, d), wa=c(wa), qn=q_norm.reshape(1, Q_RANK),
        kvn=kv_norm.reshape(1, KV_RANK), wqn=c(wqn), wqa=c(wqa), wqb=c(wqb),
        wukt=c(jnp.transpose(w_uk, (1, 2, 0))), wuv=c(jnp.transpose(w_uv, (1, 0, 2))),
        wo=c(w_o))


def _mla(x, p):
    b, s, d = x.shape
    cos, sin = _rope_tables(s)
    tm = _tile(s, 512)
    q, k = pl.pallas_call(
        _mla_proj_kernel,
        grid=(b, s // tm),
        in_specs=[
            pl.BlockSpec((1, tm, d), lambda bi, i: (bi, i, 0)),
            _resident((1, d)), _resident(p["wa"].shape), _resident((1, Q_RANK)),
            _resident((1, KV_RANK)), _resident(p["wqn"].shape), _resident(p["wqa"].shape),
            _resident(p["wqb"].shape), _resident(p["wukt"].shape),
            pl.BlockSpec((tm, ROPE_PAD), lambda bi, i: (i, 0)),
            pl.BlockSpec((tm, ROPE_PAD), lambda bi, i: (i, 0)),
        ],
        out_specs=[
            pl.BlockSpec((1, MLA_HEADS, tm, QK_WIDTH), lambda bi, i: (bi, 0, i, 0)),
            pl.BlockSpec((1, tm, QK_WIDTH), lambda bi, i: (bi, i, 0)),
        ],
        out_shape=[
            jax.ShapeDtypeStruct((b, MLA_HEADS, s, QK_WIDTH), MXU_DTYPE),
            jax.ShapeDtypeStruct((b, s, QK_WIDTH), MXU_DTYPE),
        ],
        compiler_params=_params("parallel", "parallel"),
        name="mla_proj",
    )(x, p["g"], p["wa"], p["qn"], p["kvn"], p["wqn"], p["wqa"], p["wqb"], p["wukt"], cos, sin)

    tq = _tile(s, 128)
    tk = _tile(s // 2, 512)
    rows = MLA_HEADS * tq
    return pl.pallas_call(
        functools.partial(_mla_attn_kernel, tk=tk, nk=s // tk),
        grid=(b, s // tq),
        in_specs=[
            pl.BlockSpec((1, MLA_HEADS, tq, QK_WIDTH), lambda bi, i: (bi, 0, i, 0)),
            pl.BlockSpec((1, s, QK_WIDTH), lambda bi, i: (bi, 0, 0)),
            pl.BlockSpec((1, tq, d), lambda bi, i: (bi, i, 0)),
            _resident(p["wuv"].shape), _resident(p["wo"].shape),
        ],
        out_specs=pl.BlockSpec((1, tq, d), lambda bi, i: (bi, i, 0)),
        out_shape=jax.ShapeDtypeStruct((b, s, d), jnp.float32),
        scratch_shapes=[pltpu.VMEM((rows, LANES), jnp.float32),
                        pltpu.VMEM((rows, LANES), jnp.float32),
                        pltpu.VMEM((rows, KV_RANK), jnp.float32),
                        pltpu.VMEM((2, rows, tk), jnp.float32)],
        compiler_params=_params("parallel", "arbitrary"),
        name="mla_attn",
    )(q, k, x, p["wuv"], p["wo"])


def _ssm_in_kernel(x_ref, g_ref, wz_ref, wx_ref, wdt_ref, dtb_ref, z_ref, xbc_ref, dt_ref):
    h = _rms(x_ref[0], g_ref[...]).astype(MXU_DTYPE)
    z_ref[0] = _dot_nt(wz_ref[...], h)
    xbc_ref[0] = _dot_nt(wx_ref[...], h)
    dt_ref[0] = jax.nn.softplus(_dot_nt(wdt_ref[...], h) + dtb_ref[...])


def _conv_kernel(xm_ref, xp_ref, xn_ref, w_ref, b_ref, o_ref, *, nt):
    i = pl.program_id(2)
    xm = xm_ref[0]
    tl = xm.shape[1]
    xp = jnp.where(i > 0, xp_ref[0], 0.0)
    xn = jnp.where(i < nt - 1, xn_ref[0], 0.0)
    ext = jnp.concatenate([xp, xm, xn], axis=1)
    width = tl + 2 * LANES
    w = w_ref[...]
    acc = b_ref[...] + w[:, CONV_K // 2:CONV_K // 2 + 1] * xm
    for k in range(CONV_K):
        d = k - CONV_K // 2
        if d == 0:
            continue
        shifted = pltpu.roll(ext, (width - d) % width, axis=1)[:, LANES:LANES + tl]
        acc = acc + w[:, k:k + 1] * shifted
    o_ref[0] = _silu(acc)


def _ssd_chunks(xs_ref, b_ref, c_ref, dt_ref, alog_ref, y_ref, st_ref, *, rev, first):
    q = LANES
    tl = xs_ref.shape[2]
    nch = tl // q
    hp = HEADS_PER_GROUP * SSM_HEADDIM

    @pl.when(first)
    def _():
        st_ref[...] = jnp.zeros(st_ref.shape, jnp.float32)

    row = lax.broadcasted_iota(jnp.int32, (q, q), 0)
    col = lax.broadcasted_iota(jnp.int32, (q, q), 1)
    mask = (row >= col) if rev else (row <= col)
    u_sl = mask.astype(jnp.float32)
    u_ls = ((col >= row) if rev else (col <= row)).astype(jnp.float32)
    ones = jnp.ones((q, LANES), jnp.float32)
    a = -jnp.exp(alog_ref[...])

    order = range(nch - 1, -1, -1) if rev else range(nch)
    for ci in order:
        lanes = slice(ci * q, (ci + 1) * q)
        dt = dt_ref[0, :, lanes]
        la = dt * a
        cs = _dot_f32(la, u_sl)
        cs_t = _dot_f32(u_ls, la, _NT)
        tot = _dot_f32(la, ones)
        ecs = jnp.exp(cs)
        dte = jnp.exp(tot - cs)
        etot = jnp.exp(tot)
        for g in range(SSM_GROUPS):
            grows = slice(g * D_STATE, (g + 1) * D_STATE)
            bt = b_ref[0, grows, lanes]
            ct = c_ref[0, grows, lanes].astype(MXU_DTYPE)
            bct = _dot(bt.T.astype(MXU_DTYPE), ct)
            bt = bt.astype(MXU_DTYPE)
            r0 = g * hp
            prev = st_ref[r0:r0 + hp, :]
            y_off = _dot(prev.astype(MXU_DTYPE), ct)
            xdte = []
            for r in range(HEADS_PER_GROUP):
                hd = g * HEADS_PER_GROUP + r
                rows = slice(r0 + r * SSM_HEADDIM, r0 + (r + 1) * SSM_HEADDIM)
                xdt = xs_ref[0, rows, lanes] * dt[hd:hd + 1, :]
                seg = cs[hd:hd + 1, :] - cs_t[:, hd:hd + 1]
                mt = bct * jnp.exp(jnp.where(mask, seg, NEG_BIG))
                y_diag = _dot(xdt.astype(MXU_DTYPE), mt.astype(MXU_DTYPE))
                y_ref[rows, lanes] = (y_diag + y_off[r * SSM_HEADDIM:(r + 1) * SSM_HEADDIM]
                                      * ecs[hd:hd + 1, :])
                xdte.append(xdt * dte[hd:hd + 1, :])
                st_ref[rows, :] = prev[r * SSM_HEADDIM:(r + 1) * SSM_HEADDIM] * etot[hd:hd + 1, :]
            states = _dot_nt(jnp.concatenate(xdte, axis=0).astype(MXU_DTYPE), bt)
            st_ref[r0:r0 + hp, :] = st_ref[r0:r0 + hp, :] + states


def _ssd_fwd_kernel(xs_ref, b_ref, c_ref, dt_ref, alog_ref, y_ref, st_ref):
    _ssd_chunks(xs_ref, b_ref, c_ref, dt_ref, alog_ref, y_ref.at[0], st_ref,
                rev=False, first=pl.program_id(1) == 0)


def _ssd_bwd_kernel(xs_ref, b_ref, c_ref, dt_ref, alog_ref, yf_ref, z_ref, dcol_ref, ng_ref,
                    wout_ref, x_ref, o_ref, st_ref, yb_ref):
    _ssd_chunks(xs_ref, b_ref, c_ref, dt_ref, alog_ref, yb_ref, st_ref,
                rev=True, first=pl.program_id(1) == 0)
    y = yf_ref[0] + yb_ref[...] + xs_ref[0] * dcol_ref[...]
    y = y * _silu(z_ref[0])
    gsz = y.shape[0] // SSM_GROUPS
    parts = []
    for g in range(SSM_GROUPS):
        yg = y[g * gsz:(g + 1) * gsz]
        parts.append(yg * lax.rsqrt(jnp.mean(yg * yg, axis=0, keepdims=True) + EPS))
    y = (jnp.concatenate(parts, axis=0) * ng_ref[...]).astype(MXU_DTYPE)
    out_t = _dot(wout_ref[...], y)
    o_ref[0] = x_ref[0] + out_t.T


def _prep_ssm(w_in, conv_w, conv_b, dt_bias, a_log, d_skip, norm_g, w_out, mix_norm):
    d = w_in.shape[0]
    d_inner = SSM_HEADS * SSM_HEADDIM
    conv_dim = d_inner + 2 * SSM_GROUPS * D_STATE
    w_t = w_in.T.astype(MXU_DTYPE)
    return dict(
        g=mix_norm.reshape(1, d), wz=w_t[:d_inner], wx=w_t[d_inner:d_inner + conv_dim],
        wdt=w_t[d_inner + conv_dim:], dtb=dt_bias.reshape(2 * SSM_HEADS, 1),
        conv_w=conv_w.T, conv_b=conv_b.reshape(conv_dim, 1),
        alog=a_log.reshape(2, SSM_HEADS, 1),
        dcol=jnp.repeat(d_skip, SSM_HEADDIM).reshape(d_inner, 1),
        ng=norm_g.reshape(d_inner, 1), wout=w_out.T.astype(MXU_DTYPE))


def _mamba(x, p):
    b, l, d = x.shape
    d_inner = SSM_HEADS * SSM_HEADDIM
    gn = SSM_GROUPS * D_STATE
    conv_dim = d_inner + 2 * gn
    nh2 = 2 * SSM_HEADS

    tl = _tile(l, 256)
    z_t, xbc_t, dt_t = pl.pallas_call(
        _ssm_in_kernel,
        grid=(b, l // tl),
        in_specs=[
            pl.BlockSpec((1, tl, d), lambda bi, i: (bi, i, 0)),
            _resident((1, d)), _resident((d_inner, d)), _resident((conv_dim, d)),
            _resident((nh2, d)), _resident((nh2, 1)),
        ],
        out_specs=[
            pl.BlockSpec((1, d_inner, tl), lambda bi, i: (bi, 0, i)),
            pl.BlockSpec((1, conv_dim, tl), lambda bi, i: (bi, 0, i)),
            pl.BlockSpec((1, nh2, tl), lambda bi, i: (bi, 0, i)),
        ],
        out_shape=[
            jax.ShapeDtypeStruct((b, d_inner, l), jnp.float32),
            jax.ShapeDtypeStruct((b, conv_dim, l), jnp.float32),
            jax.ShapeDtypeStruct((b, nh2, l), jnp.float32),
        ],
        compiler_params=_params("parallel", "parallel"),
        name="ssm_in",
    )(x, p["g"], p["wz"], p["wx"], p["wdt"], p["dtb"])

    ct = _tile(conv_dim, 512)
    tc = _tile(l, 1024)
    nt = l // tc
    per = tc // LANES
    last_halo = l // LANES - 1
    xbc_c = pl.pallas_call(
        functools.partial(_conv_kernel, nt=nt),
        grid=(b, conv_dim // ct, nt),
        in_specs=[
            pl.BlockSpec((1, ct, tc), lambda bi, ci, i: (bi, ci, i)),
            pl.BlockSpec((1, ct, LANES), lambda bi, ci, i: (bi, ci, jnp.maximum(i * per - 1, 0))),
            pl.BlockSpec((1, ct, LANES),
                         lambda bi, ci, i: (bi, ci, jnp.minimum((i + 1) * per, last_halo))),
            pl.BlockSpec((ct, CONV_K), lambda bi, ci, i: (ci, 0)),
            pl.BlockSpec((ct, 1), lambda bi, ci, i: (ci, 0)),
        ],
        out_specs=pl.BlockSpec((1, ct, tc), lambda bi, ci, i: (bi, ci, i)),
        out_shape=jax.ShapeDtypeStruct((b, conv_dim, l), jnp.float32),
        compiler_params=_params("parallel", "parallel", "parallel"),
        name="ssm_conv",
    )(xbc_t, xbc_t, xbc_t, p["conv_w"], p["conv_b"])

    ts = _tile(l, 256)
    nb = l // ts
    b_blk = d_inner // gn
    st_shape = pltpu.VMEM((d_inner, D_STATE), jnp.float32)

    def scan_specs(tmap):
        return [
            pl.BlockSpec((1, d_inner, ts), lambda bi, i: (bi, 0, tmap(i))),
            pl.BlockSpec((1, gn, ts), lambda bi, i: (bi, b_blk, tmap(i))),
            pl.BlockSpec((1, gn, ts), lambda bi, i: (bi, b_blk + 1, tmap(i))),
        ]

    y_f = pl.pallas_call(
        _ssd_fwd_kernel,
        grid=(b, nb),
        in_specs=scan_specs(lambda i: i) + [
            pl.BlockSpec((1, SSM_HEADS, ts), lambda bi, i: (bi, 0, i)),
            pl.BlockSpec((None, SSM_HEADS, 1), lambda bi, i: (0, 0, 0)),
        ],
        out_specs=pl.BlockSpec((1, d_inner, ts), lambda bi, i: (bi, 0, i)),
        out_shape=jax.ShapeDtypeStruct((b, d_inner, l), jnp.float32),
        scratch_shapes=[st_shape],
        compiler_params=_params("parallel", "arbitrary"),
        name="ssd_fwd",
    )(xbc_c, xbc_c, xbc_c, dt_t, p["alog"])

    rmap = lambda i: nb - 1 - i
    return pl.pallas_call(
        _ssd_bwd_kernel,
        grid=(b, nb),
        in_specs=scan_specs(rmap) + [
            pl.BlockSpec((1, SSM_HEADS, ts), lambda bi, i: (bi, 1, rmap(i))),
            pl.BlockSpec((None, SSM_HEADS, 1), lambda bi, i: (1, 0, 0)),
            pl.BlockSpec((1, d_inner, ts), lambda bi, i: (bi, 0, rmap(i))),
            pl.BlockSpec((1, d_inner, ts), lambda bi, i: (bi, 0, rmap(i))),
            _resident((d_inner, 1)), _resident((d_inner, 1)), _resident((d, d_inner)),
            pl.BlockSpec((1, ts, d), lambda bi, i: (bi, rmap(i), 0)),
        ],
        out_specs=pl.BlockSpec((1, ts, d), lambda bi, i: (bi, rmap(i), 0)),
        out_shape=jax.ShapeDtypeStruct((b, l, d), jnp.float32),
        scratch_shapes=[st_shape, pltpu.VMEM((d_inner, ts), jnp.float32)],
        compiler_params=_params("parallel", "arbitrary"),
        name="ssd_bwd",
    )(xbc_c, xbc_c, xbc_c, dt_t, p["alog"], y_f, z_t, p["dcol"], p["ng"], p["wout"], x)


def kernel(x_prompt, x_sample, ffn1_norm, ffn1_w_in, ffn1_w_out, mix_norm, ffn2_norm, ffn2_w_in, ffn2_w_out, mla_w_a, mla_q_norm, mla_kv_norm, mla_w_uq, mla_w_uk, mla_w_uv, mla_w_o, ssm_w_in, ssm_conv_w, ssm_conv_b, ssm_dt_bias, ssm_a_log, ssm_d, ssm_norm, ssm_w_out, final_norm):
    depth = ffn1_norm.shape[0]
    n_mixers = 2
    c = lambda w: w.astype(MXU_DTYPE)
    w1_in, w1_out, w2_in, w2_out = c(ffn1_w_in), c(ffn1_w_out), c(ffn2_w_in), c(ffn2_w_out)
    mixers = []
    for i in range(depth):
        j = i // n_mixers
        if i % n_mixers == 0:
            mixers.append(_prep_mla(mla_w_a[j], mla_q_norm[j], mla_kv_norm[j], mla_w_uq[j],
                                    mla_w_uk[j], mla_w_uv[j], mla_w_o[j], mix_norm[i]))
        else:
            mixers.append(_prep_ssm(ssm_w_in[j], ssm_conv_w[j], ssm_conv_b[j], ssm_dt_bias[j],
                                    ssm_a_log[j], ssm_d[j], ssm_norm[j], ssm_w_out[j],
                                    mix_norm[i]))

    def trunk(x):
        b, s, d = x.shape
        for i in range(depth):
            x = _ffn(x.reshape(b * s, d), ffn1_norm[i], w1_in[i], w1_out[i]).reshape(b, s, d)
            x = _mla(x, mixers[i]) if i % n_mixers == 0 else _mamba(x, mixers[i])
            fg = final_norm if i == depth - 1 else None
            x = _ffn(x.reshape(b * s, d), ffn2_norm[i], w2_in[i], w2_out[i], fg).reshape(b, s, d)
        return x

    return (trunk(x_prompt), trunk(x_sample))
```

```python
import functools

import jax
import jax.numpy as jnp
from jax import lax
from jax.experimental import pallas as pl
from jax.experimental.pallas import tpu as pltpu

EPS = 1e-6
MLA_HEADS = 8
QK_NOPE = 128
QK_ROPE = 64
V_DIM = 128
Q_RANK = 256
KV_RANK = 256
ROPE_THETA = 10000.0
SOFTMAX_SCALE = (QK_NOPE + QK_ROPE) ** -0.5
LOG2E = 1.4426950408889634
QK_SCALE = SOFTMAX_SCALE * LOG2E
SSM_HEADDIM = 64
SSM_HEADS = 32
SSM_GROUPS = 4
HEADS_PER_GROUP = SSM_HEADS // SSM_GROUPS
D_STATE = 128
CONV_K = 5

LANES = 128
HALO = 8
VMEM_LIMIT_BYTES = 56 * 1024 * 1024

MXU_DTYPE = jnp.bfloat16
ROPE_PAD = LANES
QK_WIDTH = KV_RANK + ROPE_PAD
NEG_BIG = -1e30

_NT = (((1,), (1,)), ((), ()))


def _dot(a, b):
    return jnp.dot(a, b, preferred_element_type=jnp.float32)


def _dot_nt(a, b):
    return lax.dot_general(a, b, _NT, preferred_element_type=jnp.float32)


def _dot_f32(a, b, dims=None):
    if dims is None:
        return jnp.dot(a, b, preferred_element_type=jnp.float32,
                       precision=lax.Precision.HIGHEST)
    return lax.dot_general(a, b, dims, preferred_element_type=jnp.float32,
                           precision=lax.Precision.HIGHEST)


def _rms(x, g):
    return x * lax.rsqrt(jnp.mean(x * x, axis=-1, keepdims=True) + EPS) * g


def _silu(x):
    return x * jax.nn.sigmoid(x)


def _params(*sem):
    return pltpu.CompilerParams(dimension_semantics=sem, vmem_limit_bytes=VMEM_LIMIT_BYTES)


def _tile(n, want):
    t = min(n, want)
    assert n % t == 0, (n, t)
    return t


def _resident(shape):
    nd = len(shape)
    return pl.BlockSpec(shape, lambda *_: (0,) * nd, pipeline_mode=pl.Buffered(1))


def _ffn_kernel(*refs, n_chunks, chunk, final):
    if final:
        x_ref, g_ref, wg_ref, wu_ref, wo_ref, fg_ref, o_ref, a_ref = refs
    else:
        x_ref, g_ref, wg_ref, wu_ref, wo_ref, o_ref, a_ref = refs
    x = x_ref[...]
    h = _rms(x, g_ref[...]).astype(MXU_DTYPE)
    for c in range(n_chunks):
        sl = slice(c * chunk, (c + 1) * chunk)
        gate = _dot(h, wg_ref[:, sl])
        up = _dot(h, wu_ref[:, sl])
        a_ref[:, sl] = (_silu(gate) * up).astype(MXU_DTYPE)
    y = x + 0.5 * _dot(a_ref[...], wo_ref[...])
    if final:
        y = _rms(y, fg_ref[...])
    o_ref[...] = y


def _ffn(x, norm_g, w_in, w_out, final_g=None):
    t, d = x.shape
    f = w_out.shape[0]
    tm = _tile(t, 512)
    chunk = 2 * LANES if f % (2 * LANES) == 0 else f
    final = final_g is not None
    in_specs = [
        pl.BlockSpec((tm, d), lambda i: (i, 0)),
        _resident((1, d)),
        pl.BlockSpec((d, f), lambda i: (0, 0), pipeline_mode=pl.Buffered(1)),
        pl.BlockSpec((d, f), lambda i: (0, 1), pipeline_mode=pl.Buffered(1)),
        _resident((f, d)),
    ]
    args = [x, norm_g.reshape(1, d), w_in, w_in, w_out]
    if final:
        in_specs.append(_resident((1, d)))
        args.append(final_g.reshape(1, d))
    return pl.pallas_call(
        functools.partial(_ffn_kernel, n_chunks=f // chunk, chunk=chunk, final=final),
        grid=(t // tm,),
        in_specs=in_specs,
        out_specs=pl.BlockSpec((tm, d), lambda i: (i, 0)),
        out_shape=jax.ShapeDtypeStruct((t, d), jnp.float32),
        scratch_shapes=[pltpu.VMEM((tm, f), MXU_DTYPE)],
        compiler_params=_params("parallel"),
        name="ffn_final" if final else "ffn",
    )(*args)


def _mla_proj_kernel(x_ref, g_ref, wa_ref, qn_ref, kvn_ref, wqn_ref, wqa_ref, wqb_ref,
                     wukt_ref, cos_ref, sin_ref, q_ref, k_ref):
    h = _rms(x_ref[0], g_ref[...]).astype(MXU_DTYPE)
    c = _dot(h, wa_ref[...])
    c_q = _rms(c[:, :Q_RANK], qn_ref[...]).astype(MXU_DTYPE)
    c_kv = _rms(c[:, Q_RANK:Q_RANK + KV_RANK], kvn_ref[...])
    cos = cos_ref[...]
    sin = sin_ref[...]
    k0 = Q_RANK + KV_RANK
    k_rope = c[:, k0:k0 + ROPE_PAD] * cos + c[:, k0 + ROPE_PAD:k0 + 2 * ROPE_PAD] * sin
    k_ref[0, :, :KV_RANK] = c_kv.astype(MXU_DTYPE)
    k_ref[0, :, KV_RANK:] = k_rope.astype(MXU_DTYPE)
    q_nope = _dot(c_q, wqn_ref[...]).astype(MXU_DTYPE)
    q_a = _dot(c_q, wqa_ref[...])
    q_b = _dot(c_q, wqb_ref[...])
    for hd in range(MLA_HEADS):
        q_lat = _dot(q_nope[:, hd * QK_NOPE:(hd + 1) * QK_NOPE], wukt_ref[hd])
        sl = slice(hd * ROPE_PAD, (hd + 1) * ROPE_PAD)
        q_rope = q_a[:, sl] * cos + q_b[:, sl] * sin
        q_ref[0, hd, :, :KV_RANK] = (q_lat * QK_SCALE).astype(MXU_DTYPE)
        q_ref[0, hd, :, KV_RANK:] = (q_rope * QK_SCALE).astype(MXU_DTYPE)


def _mla_attn_kernel(q_ref, k_ref, x_ref, wuv_ref, wo_ref, o_ref, m_ref, l_ref, acc_ref,
                     s_ref, *, tk, nk):
    nh, tq = q_ref.shape[1], q_ref.shape[2]
    q = q_ref[0].reshape(nh * tq, QK_WIDTH)
    m_ref[...] = jnp.full(m_ref.shape, -jnp.inf, jnp.float32)
    l_ref[...] = jnp.zeros(l_ref.shape, jnp.float32)
    acc_ref[...] = jnp.zeros(acc_ref.shape, jnp.float32)
    ntile = tk // LANES

    def scores(kt):
        return _dot_nt(q, k_ref[0, pl.ds(pl.multiple_of(kt * tk, tk), tk), :])

    def consume(slot, kt):
        s = s_ref[slot]
        tiles = [s[:, j * LANES:(j + 1) * LANES] for j in range(ntile)]
        m_prev = m_ref[...]
        m_new = jnp.maximum(m_prev, jnp.max(functools.reduce(jnp.maximum, tiles),
                                            axis=1, keepdims=True))
        p = [jnp.exp2(t - m_new) for t in tiles]
        alpha = jnp.exp2(m_prev - m_new)
        l_ref[...] = alpha * l_ref[...] + functools.reduce(jnp.add, p)
        m_ref[...] = m_new
        v = k_ref[0, pl.ds(pl.multiple_of(kt * tk, tk), tk), :KV_RANK]
        pv = _dot(jnp.concatenate(p, axis=1).astype(MXU_DTYPE), v)
        acc_ref[...] = acc_ref[...] * jnp.concatenate([alpha] * (KV_RANK // LANES), axis=1) + pv

    s_ref[0] = scores(0)

    def pair(j, carry):
        t0 = 2 * j
        s_ref[1] = scores(t0 + 1)
        consume(0, t0)
        s_ref[0] = scores(t0 + 2)
        consume(1, t0 + 1)
        return carry

    lax.fori_loop(0, nk // 2 - 1, pair, 0)
    s_ref[1] = scores(nk - 1)
    consume(0, nk - 2)
    consume(1, nk - 1)
    l = jnp.sum(l_ref[...], axis=1, keepdims=True)
    o_lat = (acc_ref[...] / l).astype(MXU_DTYPE)
    heads = [_dot(o_lat[hd * tq:(hd + 1) * tq], wuv_ref[hd]) for hd in range(nh)]
    o = jnp.concatenate(heads, axis=1).astype(MXU_DTYPE)
    o_ref[0] = x_ref[0] + _dot(o, wo_ref[...])


def _rope_tables(seq):
    inv = ROPE_THETA ** (-jnp.arange(0, QK_ROPE, 2, dtype=jnp.float32) / QK_ROPE)
    ang = jnp.arange(seq, dtype=jnp.float32)[:, None] * inv[None, :]
    pad = jnp.zeros((seq, ROPE_PAD - QK_ROPE), jnp.float32)
    cos, sin = jnp.cos(ang), jnp.sin(ang)
    return (jnp.concatenate([cos, cos, pad], axis=1), jnp.concatenate([sin, sin, pad], axis=1))


def _prep_mla(w_a, q_norm, kv_norm, w_uq, w_uk, w_uv, w_o, mix_norm):
    d = w_a.shape[0]
    half = QK_ROPE // 2
    k0 = Q_RANK + KV_RANK
    zpad = jnp.zeros((d, ROPE_PAD - QK_ROPE), w_a.dtype)
    wa = jnp.concatenate([w_a, zpad, -w_a[:, k0 + half:], w_a[:, k0:k0 + half], zpad], axis=1)
    r1 = w_uq[:, :, QK_NOPE:QK_NOPE + half]
    r2 = w_uq[:, :, QK_NOPE + half:]
    zq = jnp.zeros((Q_RANK, MLA_HEADS, ROPE_PAD - QK_ROPE), w_uq.dtype)
    wqa = jnp.concatenate([r1, r2, zq], axis=2).reshape(Q_RANK, MLA_HEADS * ROPE_PAD)
    wqb = jnp.concatenate([-r2, r1, zq], axis=2).reshape(Q_RANK, MLA_HEADS * ROPE_PAD)
    wqn = w_uq[:, :, :QK_NOPE].reshape(Q_RANK, MLA_HEADS * QK_NOPE)
    c = lambda w: w.astype(MXU_DTYPE)
    return dict(
        g=mix_norm.reshape(1, d), wa=c(wa), qn=q_norm.reshape(1, Q_RANK),
        kvn=kv_norm.reshape(1, KV_RANK), wqn=c(wqn), wqa=c(wqa), wqb=c(wqb),
        wukt=c(jnp.transpose(w_uk, (1, 2, 0))), wuv=c(jnp.transpose(w_uv, (1, 0, 2))),
        wo=c(w_o))


def _mla(x, p):
    b, s, d = x.shape
    cos, sin = _rope_tables(s)
    tm = _tile(s, 512)
    q, k = pl.pallas_call(
        _mla_proj_kernel,
        grid=(b, s // tm),
        in_specs=[
            pl.BlockSpec((1, tm, d), lambda bi, i: (bi, i, 0)),
            _resident((1, d)), _resident(p["wa"].shape), _resident((1, Q_RANK)),
            _resident((1, KV_RANK)), _resident(p["wqn"].shape), _resident(p["wqa"].shape),
            _resident(p["wqb"].shape), _resident(p["wukt"].shape),
            pl.BlockSpec((tm, ROPE_PAD), lambda bi, i: (i, 0)),
            pl.BlockSpec((tm, ROPE_PAD), lambda bi, i: (i, 0)),
        ],
        out_specs=[
            pl.BlockSpec((1, MLA_HEADS, tm, QK_WIDTH), lambda bi, i: (bi, 0, i, 0)),
            pl.BlockSpec((1, tm, QK_WIDTH), lambda bi, i: (bi, i, 0)),
        ],
        out_shape=[
            jax.ShapeDtypeStruct((b, MLA_HEADS, s, QK_WIDTH), MXU_DTYPE),
            jax.ShapeDtypeStruct((b, s, QK_WIDTH), MXU_DTYPE),
        ],
        compiler_params=_params("parallel", "parallel"),
        name="mla_proj",
    )(x, p["g"], p["wa"], p["qn"], p["kvn"], p["wqn"], p["wqa"], p["wqb"], p["wukt"], cos, sin)

    tq = _tile(s, 128)
    tk = _tile(s // 2, 512)
    rows = MLA_HEADS * tq
    return pl.pallas_call(
        functools.partial(_mla_attn_kernel, tk=tk, nk=s // tk),
        grid=(b, s // tq),
        in_specs=[
            pl.BlockSpec((1, MLA_HEADS, tq, QK_WIDTH), lambda bi, i: (bi, 0, i, 0)),
            pl.BlockSpec((1, s, QK_WIDTH), lambda bi, i: (bi, 0, 0)),
            pl.BlockSpec((1, tq, d), lambda bi, i: (bi, i, 0)),
            _resident(p["wuv"].shape), _resident(p["wo"].shape),
        ],
        out_specs=pl.BlockSpec((1, tq, d), lambda bi, i: (bi, i, 0)),
        out_shape=jax.ShapeDtypeStruct((b, s, d), jnp.float32),
        scratch_shapes=[pltpu.VMEM((rows, LANES), jnp.float32),
                        pltpu.VMEM((rows, LANES), jnp.float32),
                        pltpu.VMEM((rows, KV_RANK), jnp.float32),
                        pltpu.VMEM((2, rows, tk), jnp.float32)],
        compiler_params=_params("parallel", "arbitrary"),
        name="mla_attn",
    )(q, k, x, p["wuv"], p["wo"])


def _ssm_in_kernel(xm_ref, xp_ref, xn_ref, g_ref, wz_ref, wx_ref, wdt_ref, dtb_ref, cw_ref,
                   cb_ref, z_ref, xc_ref, b_ref, dt_ref, ext_ref, *, nt):
    i = pl.program_id(1)
    g = g_ref[...]
    tl = xm_ref.shape[1]
    d_inner = z_ref.shape[1]
    gn = b_ref.shape[2]
    hm = _rms(xm_ref[0], g).astype(MXU_DTYPE)
    z_ref[0] = _dot_nt(wz_ref[...], hm)
    dt_ref[0] = jax.nn.softplus(_dot_nt(wdt_ref[...], hm) + dtb_ref[...])
    xp = jnp.where(i > 0, xp_ref[0], 0.0)
    xn = jnp.where(i < nt - 1, xn_ref[0], 0.0)
    h_ext = _rms(jnp.concatenate([xp, xm_ref[0], xn], axis=0), g).astype(MXU_DTYPE)
    ext_ref[...] = _dot(h_ext, wx_ref[...])
    for c in range(ext_ref.shape[1] // LANES):
        cols = slice(c * LANES, (c + 1) * LANES)
        ext = ext_ref[:, cols]
        n_ext = ext.shape[0]
        acc = cb_ref[:, cols]
        for k in range(CONV_K):
            d = k - CONV_K // 2
            tap = ext if d == 0 else pltpu.roll(ext, (n_ext - d) % n_ext, axis=0)
            acc = acc + cw_ref[k:k + 1, cols] * tap[HALO:HALO + tl]
        y = _silu(acc)
        lo = c * LANES
        if lo < d_inner:
            xc_ref[0, lo:lo + LANES, :] = y.T
        elif lo < d_inner + gn:
            b_ref[0, :, lo - d_inner:lo - d_inner + LANES] = y
        else:
            xc_ref[0, lo - gn:lo - gn + LANES, :] = y.T


def _ssd_chunks(xc_ref, b_ref, dt_ref, alog_ref, y_ref, st_ref, *, rev, first):
    q = LANES
    tl = dt_ref.shape[2]
    nch = tl // q
    hp = HEADS_PER_GROUP * SSM_HEADDIM
    d_inner = SSM_HEADS * SSM_HEADDIM

    @pl.when(first)
    def _():
        st_ref[...] = jnp.zeros(st_ref.shape, jnp.float32)

    row = lax.broadcasted_iota(jnp.int32, (q, q), 0)
    col = lax.broadcasted_iota(jnp.int32, (q, q), 1)
    mask = (row >= col) if rev else (row <= col)
    u_sl = mask.astype(jnp.float32)
    u_ls = ((col >= row) if rev else (col <= row)).astype(jnp.float32)
    ones = jnp.ones((q, LANES), jnp.float32)
    a = -jnp.exp(alog_ref[...]) * LOG2E

    order = range(nch - 1, -1, -1) if rev else range(nch)
    for ci in order:
        lanes = slice(ci * q, (ci + 1) * q)
        dt = dt_ref[0, :, lanes]
        la = dt * a
        cs = _dot_f32(la, u_sl)
        cs_t = _dot_f32(u_ls, la, _NT)
        tot = _dot_f32(la, ones)
        ecs = jnp.exp2(cs)
        dte = jnp.exp2(tot - cs)
        etot = jnp.exp2(tot)
        for g in range(SSM_GROUPS):
            gcols = slice(g * D_STATE, (g + 1) * D_STATE)
            bm = b_ref[0, lanes, gcols].astype(MXU_DTYPE)
            ct = xc_ref[0, d_inner + g * D_STATE:d_inner + (g + 1) * D_STATE, lanes]
            ct = ct.astype(MXU_DTYPE)
            bct = _dot(bm, ct)
            r0 = g * hp
            prev = st_ref[r0:r0 + hp, :]
            y_off = _dot(prev.astype(MXU_DTYPE), ct)
            xdte = []
            for r in range(HEADS_PER_GROUP):
                hd = g * HEADS_PER_GROUP + r
                rows = slice(r0 + r * SSM_HEADDIM, r0 + (r + 1) * SSM_HEADDIM)
                xdt = xc_ref[0, rows, lanes] * dt[hd:hd + 1, :]
                seg = cs[hd:hd + 1, :] - cs_t[:, hd:hd + 1]
                mt = bct * jnp.exp2(jnp.where(mask, seg, NEG_BIG))
                y_diag = _dot(xdt.astype(MXU_DTYPE), mt.astype(MXU_DTYPE))
                y_ref[rows, lanes] = (y_diag + y_off[r * SSM_HEADDIM:(r + 1) * SSM_HEADDIM]
                                      * ecs[hd:hd + 1, :])
                xdte.append(xdt * dte[hd:hd + 1, :])
                st_ref[rows, :] = prev[r * SSM_HEADDIM:(r + 1) * SSM_HEADDIM] * etot[hd:hd + 1, :]
            states = _dot(jnp.concatenate(xdte, axis=0).astype(MXU_DTYPE), bm)
            st_ref[r0:r0 + hp, :] = st_ref[r0:r0 + hp, :] + states


def _ssd_fwd_kernel(xc_ref, b_ref, dt_ref, alog_ref, y_ref, st_ref):
    _ssd_chunks(xc_ref, b_ref, dt_ref, alog_ref, y_ref.at[0], st_ref,
                rev=False, first=pl.program_id(1) == 0)


def _ssd_bwd_kernel(xc_ref, b_ref, dt_ref, alog_ref, yf_ref, z_ref, dcol_ref, ng_ref,
                    wout_ref, x_ref, o_ref, st_ref, yb_ref):
    _ssd_chunks(xc_ref, b_ref, dt_ref, alog_ref, yb_ref, st_ref,
                rev=True, first=pl.program_id(1) == 0)
    d_inner = yf_ref.shape[1]
    y = yf_ref[0] + yb_ref[...] + xc_ref[0, :d_inner, :] * dcol_ref[...]
    y = y * _silu(z_ref[0])
    gsz = d_inner // SSM_GROUPS
    parts = []
    for g in range(SSM_GROUPS):
        yg = y[g * gsz:(g + 1) * gsz]
        parts.append(yg * lax.rsqrt(jnp.mean(yg * yg, axis=0, keepdims=True) + EPS))
    y = (jnp.concatenate(parts, axis=0) * ng_ref[...]).astype(MXU_DTYPE)
    out_t = _dot(wout_ref[...], y)
    o_ref[0] = x_ref[0] + out_t.T


def _prep_ssm(w_in, conv_w, conv_b, dt_bias, a_log, d_skip, norm_g, w_out, mix_norm):
    d = w_in.shape[0]
    d_inner = SSM_HEADS * SSM_HEADDIM
    conv_dim = d_inner + 2 * SSM_GROUPS * D_STATE
    w = w_in.astype(MXU_DTYPE)
    return dict(
        g=mix_norm.reshape(1, d), wz=w[:, :d_inner].T, wx=w[:, d_inner:d_inner + conv_dim],
        wdt=w[:, d_inner + conv_dim:].T, dtb=dt_bias.reshape(2 * SSM_HEADS, 1),
        conv_w=conv_w, conv_b=conv_b.reshape(1, conv_dim),
        alog=a_log.reshape(2, SSM_HEADS, 1),
        dcol=jnp.repeat(d_skip, SSM_HEADDIM).reshape(d_inner, 1),
        ng=norm_g.reshape(d_inner, 1), wout=w_out.T.astype(MXU_DTYPE))


def _mamba(x, p):
    b, l, d = x.shape
    d_inner = SSM_HEADS * SSM_HEADDIM
    gn = SSM_GROUPS * D_STATE
    conv_dim = d_inner + 2 * gn
    nh2 = 2 * SSM_HEADS

    tl = _tile(l, 256)
    nt = l // tl
    per = tl // HALO
    last_halo = l // HALO - 1
    z_t, xc_t, b_m, dt_t = pl.pallas_call(
        functools.partial(_ssm_in_kernel, nt=nt),
        grid=(b, nt),
        in_specs=[
            pl.BlockSpec((1, tl, d), lambda bi, i: (bi, i, 0)),
            pl.BlockSpec((1, HALO, d), lambda bi, i: (bi, jnp.maximum(i * per - 1, 0), 0)),
            pl.BlockSpec((1, HALO, d),
                         lambda bi, i: (bi, jnp.minimum((i + 1) * per, last_halo), 0)),
            _resident((1, d)), _resident((d_inner, d)), _resident((d, conv_dim)),
            _resident((nh2, d)), _resident((nh2, 1)), _resident((CONV_K, conv_dim)),
            _resident((1, conv_dim)),
        ],
        out_specs=[
            pl.BlockSpec((1, d_inner, tl), lambda bi, i: (bi, 0, i)),
            pl.BlockSpec((1, d_inner + gn, tl), lambda bi, i: (bi, 0, i)),
            pl.BlockSpec((1, tl, gn), lambda bi, i: (bi, i, 0)),
            pl.BlockSpec((1, nh2, tl), lambda bi, i: (bi, 0, i)),
        ],
        out_shape=[
            jax.ShapeDtypeStruct((b, d_inner, l), jnp.float32),
            jax.ShapeDtypeStruct((b, d_inner + gn, l), jnp.float32),
            jax.ShapeDtypeStruct((b, l, gn), jnp.float32),
            jax.ShapeDtypeStruct((b, nh2, l), jnp.float32),
        ],
        scratch_shapes=[pltpu.VMEM((tl + 2 * HALO, conv_dim), jnp.float32)],
        compiler_params=_params("parallel", "parallel"),
        name="ssm_in",
    )(x, x, x, p["g"], p["wz"], p["wx"], p["wdt"], p["dtb"], p["conv_w"], p["conv_b"])

    ts = _tile(l, 256)
    nb = l // ts
    st_shape = pltpu.VMEM((d_inner, D_STATE), jnp.float32)

    def scan_specs(tmap):
        return [
            pl.BlockSpec((1, d_inner + gn, ts), lambda bi, i: (bi, 0, tmap(i))),
            pl.BlockSpec((1, ts, gn), lambda bi, i: (bi, tmap(i), 0)),
        ]

    y_f = pl.pallas_call(
        _ssd_fwd_kernel,
        grid=(b, nb),
        in_specs=scan_specs(lambda i: i) + [
            pl.BlockSpec((1, SSM_HEADS, ts), lambda bi, i: (bi, 0, i)),
            pl.BlockSpec((None, SSM_HEADS, 1), lambda bi, i: (0, 0, 0)),
        ],
        out_specs=pl.BlockSpec((1, d_inner, ts), lambda bi, i: (bi, 0, i)),
        out_shape=jax.ShapeDtypeStruct((b, d_inner, l), jnp.float32),
        scratch_shapes=[st_shape],
        compiler_params=_params("parallel", "arbitrary"),
        name="ssd_fwd",
    )(xc_t, b_m, dt_t, p["alog"])

    rmap = lambda i: nb - 1 - i
    return pl.pallas_call(
        _ssd_bwd_kernel,
        grid=(b, nb),
        in_specs=scan_specs(rmap) + [
            pl.BlockSpec((1, SSM_HEADS, ts), lambda bi, i: (bi, 1, rmap(i))),
            pl.BlockSpec((None, SSM_HEADS, 1), lambda bi, i: (1, 0, 0)),
            pl.BlockSpec((1, d_inner, ts), lambda bi, i: (bi, 0, rmap(i))),
            pl.BlockSpec((1, d_inner, ts), lambda bi, i: (bi, 0, rmap(i))),
            _resident((d_inner, 1)), _resident((d_inner, 1)), _resident((d, d_inner)),
            pl.BlockSpec((1, ts, d), lambda bi, i: (bi, rmap(i), 0)),
        ],
        out_specs=pl.BlockSpec((1, ts, d), lambda bi, i: (bi, rmap(i), 0)),
        out_shape=jax.ShapeDtypeStruct((b, l, d), jnp.float32),
        scratch_shapes=[st_shape, pltpu.VMEM((d_inner, ts), jnp.float32)],
        compiler_params=_params("parallel", "arbitrary"),
        name="ssd_bwd",
    )(xc_t, b_m, dt_t, p["alog"], y_f, z_t, p["dcol"], p["ng"], p["wout"], x)


def kernel(x_prompt, x_sample, ffn1_norm, ffn1_w_in, ffn1_w_out, mix_norm, ffn2_norm, ffn2_w_in, ffn2_w_out, mla_w_a, mla_q_norm, mla_kv_norm, mla_w_uq, mla_w_uk, mla_w_uv, mla_w_o, ssm_w_in, ssm_conv_w, ssm_conv_b, ssm_dt_bias, ssm_a_log, ssm_d, ssm_norm, ssm_w_out, final_norm):
    depth = ffn1_norm.shape[0]
    n_mixers = 2
    c = lambda w: w.astype(MXU_DTYPE)
    w1_in, w1_out, w2_in, w2_out = c(ffn1_w_in), c(ffn1_w_out), c(ffn2_w_in), c(ffn2_w_out)
    mixers = []
    for i in range(depth):
        j = i // n_mixers
        if i % n_mixers == 0:
            mixers.append(_prep_mla(mla_w_a[j], mla_q_norm[j], mla_kv_norm[j], mla_w_uq[j],
                                    mla_w_uk[j], mla_w_uv[j], mla_w_o[j], mix_norm[i]))
        else:
            mixers.append(_prep_ssm(ssm_w_in[j], ssm_conv_w[j], ssm_conv_b[j], ssm_dt_bias[j],
                                    ssm_a_log[j], ssm_d[j], ssm_norm[j], ssm_w_out[j],
                                    mix_norm[i]))

    def trunk(x):
        b, s, d = x.shape
        for i in range(depth):
            x = _ffn(x.reshape(b * s, d), ffn1_norm[i], w1_in[i], w1_out[i]).reshape(b, s, d)
            x = _mla(x, mixers[i]) if i % n_mixers == 0 else _mamba(x, mixers[i])
            fg = final_norm if i == depth - 1 else None
            x = _ffn(x.reshape(b * s, d), ffn2_norm[i], w2_in[i], w2_out[i], fg).reshape(b, s, d)
        return x

    return (trunk(x_prompt), trunk(x_sample))
```

```python
import functools

import jax
import jax.numpy as jnp
from jax import lax
from jax.experimental import pallas as pl
from jax.experimental.pallas import tpu as pltpu

EPS = 1e-6
MLA_HEADS = 8
QK_NOPE = 128
QK_ROPE = 64
V_DIM = 128
Q_RANK = 256
KV_RANK = 256
ROPE_THETA = 10000.0
SOFTMAX_SCALE = (QK_NOPE + QK_ROPE) ** -0.5
LOG2E = 1.4426950408889634
QK_SCALE = SOFTMAX_SCALE * LOG2E
SSM_HEADDIM = 64
SSM_HEADS = 32
SSM_GROUPS = 4
HEADS_PER_GROUP = SSM_HEADS // SSM_GROUPS
D_STATE = 128
CONV_K = 5

LANES = 128
HALO = 8
VMEM_LIMIT_BYTES = 56 * 1024 * 1024

MXU_DTYPE = jnp.bfloat16
ROPE_PAD = LANES
QK_WIDTH = KV_RANK + ROPE_PAD
NEG_BIG = -1e30

_NT = (((1,), (1,)), ((), ()))


def _dot(a, b):
    return jnp.dot(a, b, preferred_element_type=jnp.float32)


def _dot_nt(a, b):
    return lax.dot_general(a, b, _NT, preferred_element_type=jnp.float32)


def _dot_f32(a, b, dims=None):
    if dims is None:
        return jnp.dot(a, b, preferred_element_type=jnp.float32,
                       precision=lax.Precision.HIGHEST)
    return lax.dot_general(a, b, dims, preferred_element_type=jnp.float32,
                           precision=lax.Precision.HIGHEST)


def _rms(x, g):
    return x * lax.rsqrt(jnp.mean(x * x, axis=-1, keepdims=True) + EPS) * g


def _silu(x):
    return x * jax.nn.sigmoid(x)


def _params(*sem):
    return pltpu.CompilerParams(dimension_semantics=sem, vmem_limit_bytes=VMEM_LIMIT_BYTES)


def _tile(n, want):
    t = min(n, want)
    assert n % t == 0, (n, t)
    return t


def _resident(shape):
    nd = len(shape)
    return pl.BlockSpec(shape, lambda *_: (0,) * nd, pipeline_mode=pl.Buffered(1))


def _ffn_kernel(*refs, n_chunks, chunk, final):
    if final:
        x_ref, g_ref, wg_ref, wu_ref, wo_ref, fg_ref, o_ref, a_ref = refs
    else:
        x_ref, g_ref, wg_ref, wu_ref, wo_ref, o_ref, a_ref = refs
    x = x_ref[...]
    h = _rms(x, g_ref[...]).astype(MXU_DTYPE)
    for c in range(n_chunks):
        sl = slice(c * chunk, (c + 1) * chunk)
        gate = _dot(h, wg_ref[:, sl])
        up = _dot(h, wu_ref[:, sl])
        a_ref[:, sl] = (_silu(gate) * up).astype(MXU_DTYPE)
    y = x + 0.5 * _dot(a_ref[...], wo_ref[...])
    if final:
        y = _rms(y, fg_ref[...])
    o_ref[...] = y


def _ffn(x, norm_g, w_in, w_out, final_g=None):
    t, d = x.shape
    f = w_out.shape[0]
    tm = _tile(t, 512)
    chunk = 2 * LANES if f % (2 * LANES) == 0 else f
    final = final_g is not None
    in_specs = [
        pl.BlockSpec((tm, d), lambda i: (i, 0)),
        _resident((1, d)),
        pl.BlockSpec((d, f), lambda i: (0, 0), pipeline_mode=pl.Buffered(1)),
        pl.BlockSpec((d, f), lambda i: (0, 1), pipeline_mode=pl.Buffered(1)),
        _resident((f, d)),
    ]
    args = [x, norm_g.reshape(1, d), w_in, w_in, w_out]
    if final:
        in_specs.append(_resident((1, d)))
        args.append(final_g.reshape(1, d))
    return pl.pallas_call(
        functools.partial(_ffn_kernel, n_chunks=f // chunk, chunk=chunk, final=final),
        grid=(t // tm,),
        in_specs=in_specs,
        out_specs=pl.BlockSpec((tm, d), lambda i: (i, 0)),
        out_shape=jax.ShapeDtypeStruct((t, d), jnp.float32),
        scratch_shapes=[pltpu.VMEM((tm, f), MXU_DTYPE)],
        compiler_params=_params("parallel"),
        name="ffn_final" if final else "ffn",
    )(*args)


def _mla_proj_kernel(x_ref, g_ref, wa_ref, qn_ref, kvn_ref, wqn_ref, wqa_ref, wqb_ref,
                     wukt_ref, cos_ref, sin_ref, q_ref, k_ref):
    h = _rms(x_ref[0], g_ref[...]).astype(MXU_DTYPE)
    c = _dot(h, wa_ref[...])
    c_q = _rms(c[:, :Q_RANK], qn_ref[...]).astype(MXU_DTYPE)
    c_kv = _rms(c[:, Q_RANK:Q_RANK + KV_RANK], kvn_ref[...])
    cos = cos_ref[...]
    sin = sin_ref[...]
    k0 = Q_RANK + KV_RANK
    k_rope = c[:, k0:k0 + ROPE_PAD] * cos + c[:, k0 + ROPE_PAD:k0 + 2 * ROPE_PAD] * sin
    k_ref[0, :, :KV_RANK] = c_kv.astype(MXU_DTYPE)
    k_ref[0, :, KV_RANK:] = k_rope.astype(MXU_DTYPE)
    q_nope = _dot(c_q, wqn_ref[...]).astype(MXU_DTYPE)
    q_a = _dot(c_q, wqa_ref[...])
    q_b = _dot(c_q, wqb_ref[...])
    for hd in range(MLA_HEADS):
        q_lat = _dot(q_nope[:, hd * QK_NOPE:(hd + 1) * QK_NOPE], wukt_ref[hd])
        sl = slice(hd * ROPE_PAD, (hd + 1) * ROPE_PAD)
        q_rope = q_a[:, sl] * cos + q_b[:, sl] * sin
        q_ref[0, hd, :, :KV_RANK] = (q_lat * QK_SCALE).astype(MXU_DTYPE)
        q_ref[0, hd, :, KV_RANK:] = (q_rope * QK_SCALE).astype(MXU_DTYPE)


def _mla_attn_kernel(q_ref, k_ref, x_ref, wuv_ref, wo_ref, o_ref, m_ref, l_ref, acc_ref,
                     s_ref, *, tk, nk):
    nh, tq = q_ref.shape[1], q_ref.shape[2]
    q = q_ref[0].reshape(nh * tq, QK_WIDTH)
    m_ref[...] = jnp.full(m_ref.shape, -jnp.inf, jnp.float32)
    l_ref[...] = jnp.zeros(l_ref.shape, jnp.float32)
    acc_ref[...] = jnp.zeros(acc_ref.shape, jnp.float32)
    ntile = tk // LANES

    def scores(kt):
        return _dot_nt(q, k_ref[0, pl.ds(pl.multiple_of(kt * tk, tk), tk), :])

    def consume(slot, kt):
        s = s_ref[slot]
        tiles = [s[:, j * LANES:(j + 1) * LANES] for j in range(ntile)]
        m_prev = m_ref[...]
        m_new = jnp.maximum(m_prev, jnp.max(functools.reduce(jnp.maximum, tiles),
                                            axis=1, keepdims=True))
        p = [jnp.exp2(t - m_new) for t in tiles]
        alpha = jnp.exp2(m_prev - m_new)
        l_ref[...] = alpha * l_ref[...] + functools.reduce(jnp.add, p)
        m_ref[...] = m_new
        v = k_ref[0, pl.ds(pl.multiple_of(kt * tk, tk), tk), :KV_RANK]
        pv = _dot(jnp.concatenate(p, axis=1).astype(MXU_DTYPE), v)
        acc_ref[...] = acc_ref[...] * jnp.concatenate([alpha] * (KV_RANK // LANES), axis=1) + pv

    s_ref[0] = scores(0)

    def pair(j, carry):
        t0 = 2 * j
        s_ref[1] = scores(t0 + 1)
        consume(0, t0)
        s_ref[0] = scores(t0 + 2)
        consume(1, t0 + 1)
        return carry

    lax.fori_loop(0, nk // 2 - 1, pair, 0)
    s_ref[1] = scores(nk - 1)
    consume(0, nk - 2)
    consume(1, nk - 1)
    l = jnp.sum(l_ref[...], axis=1, keepdims=True)
    o_lat = (acc_ref[...] / l).astype(MXU_DTYPE)
    heads = [_dot(o_lat[hd * tq:(hd + 1) * tq], wuv_ref[hd]) for hd in range(nh)]
    o = jnp.concatenate(heads, axis=1).astype(MXU_DTYPE)
    o_ref[0] = x_ref[0] + _dot(o, wo_ref[...])


def _rope_tables(seq):
    inv = ROPE_THETA ** (-jnp.arange(0, QK_ROPE, 2, dtype=jnp.float32) / QK_ROPE)
    ang = jnp.arange(seq, dtype=jnp.float32)[:, None] * inv[None, :]
    pad = jnp.zeros((seq, ROPE_PAD - QK_ROPE), jnp.float32)
    cos, sin = jnp.cos(ang), jnp.sin(ang)
    return (jnp.concatenate([cos, cos, pad], axis=1), jnp.concatenate([sin, sin, pad], axis=1))


def _prep_mla(w_a, q_norm, kv_norm, w_uq, w_uk, w_uv, w_o, mix_norm):
    d = w_a.shape[0]
    half = QK_ROPE // 2
    k0 = Q_RANK + KV_RANK
    zpad = jnp.zeros((d, ROPE_PAD - QK_ROPE), w_a.dtype)
    wa = jnp.concatenate([w_a, zpad, -w_a[:, k0 + half:], w_a[:, k0:k0 + half], zpad], axis=1)
    r1 = w_uq[:, :, QK_NOPE:QK_NOPE + half]
    r2 = w_uq[:, :, QK_NOPE + half:]
    zq = jnp.zeros((Q_RANK, MLA_HEADS, ROPE_PAD - QK_ROPE), w_uq.dtype)
    wqa = jnp.concatenate([r1, r2, zq], axis=2).reshape(Q_RANK, MLA_HEADS * ROPE_PAD)
    wqb = jnp.concatenate([-r2, r1, zq], axis=2).reshape(Q_RANK, MLA_HEADS * ROPE_PAD)
    wqn = w_uq[:, :, :QK_NOPE].reshape(Q_RANK, MLA_HEADS * QK_NOPE)
    c = lambda w: w.astype(MXU_DTYPE)
    return dict(
        g=mix_norm.reshape(1, d), wa=c(wa), qn=q_norm.reshape(1, Q_RANK),
        kvn=kv_norm.reshape(1, KV_RANK), wqn=c(wqn), wqa=c(wqa), wqb=c(wqb),
        wukt=c(jnp.transpose(w_uk, (1, 2, 0))), wuv=c(jnp.transpose(w_uv, (1, 0, 2))),
        wo=c(w_o))


def _mla(x, p):
    b, s, d = x.shape
    cos, sin = _rope_tables(s)
    tm = _tile(s, 512)
    q, k = pl.pallas_call(
        _mla_proj_kernel,
        grid=(b, s // tm),
        in_specs=[
            pl.BlockSpec((1, tm, d), lambda bi, i: (bi, i, 0)),
            _resident((1, d)), _resident(p["wa"].shape), _resident((1, Q_RANK)),
            _resident((1, KV_RANK)), _resident(p["wqn"].shape), _resident(p["wqa"].shape),
            _resident(p["wqb"].shape), _resident(p["wukt"].shape),
            pl.BlockSpec((tm, ROPE_PAD), lambda bi, i: (i, 0)),
            pl.BlockSpec((tm, ROPE_PAD), lambda bi, i: (i, 0)),
        ],
        out_specs=[
            pl.BlockSpec((1, MLA_HEADS, tm, QK_WIDTH), lambda bi, i: (bi, 0, i, 0)),
            pl.BlockSpec((1, tm, QK_WIDTH), lambda bi, i: (bi, i, 0)),
        ],
        out_shape=[
            jax.ShapeDtypeStruct((b, MLA_HEADS, s, QK_WIDTH), MXU_DTYPE),
            jax.ShapeDtypeStruct((b, s, QK_WIDTH), MXU_DTYPE),
        ],
        compiler_params=_params("parallel", "parallel"),
        name="mla_proj",
    )(x, p["g"], p["wa"], p["qn"], p["kvn"], p["wqn"], p["wqa"], p["wqb"], p["wukt"], cos, sin)

    tq = _tile(s, 128)
    tk = _tile(s // 2, 1024)
    rows = MLA_HEADS * tq
    return pl.pallas_call(
        functools.partial(_mla_attn_kernel, tk=tk, nk=s // tk),
        grid=(b, s // tq),
        in_specs=[
            pl.BlockSpec((1, MLA_HEADS, tq, QK_WIDTH), lambda bi, i: (bi, 0, i, 0)),
            pl.BlockSpec((1, s, QK_WIDTH), lambda bi, i: (bi, 0, 0)),
            pl.BlockSpec((1, tq, d), lambda bi, i: (bi, i, 0)),
            _resident(p["wuv"].shape), _resident(p["wo"].shape),
        ],
        out_specs=pl.BlockSpec((1, tq, d), lambda bi, i: (bi, i, 0)),
        out_shape=jax.ShapeDtypeStruct((b, s, d), jnp.float32),
        scratch_shapes=[pltpu.VMEM((rows, LANES), jnp.float32),
                        pltpu.VMEM((rows, LANES), jnp.float32),
                        pltpu.VMEM((rows, KV_RANK), jnp.float32),
                        pltpu.VMEM((2, rows, tk), jnp.float32)],
        compiler_params=_params("parallel", "arbitrary"),
        name="mla_attn",
    )(q, k, x, p["wuv"], p["wo"])


def _ssm_in_kernel(xm_ref, xp_ref, xn_ref, g_ref, wz_ref, wx_ref, wdt_ref, dtb_ref, cw_ref,
                   cb_ref, z_ref, xc_ref, b_ref, dt_ref, ext_ref, *, nt):
    i = pl.program_id(1)
    g = g_ref[...]
    tl = xm_ref.shape[1]
    d_inner = z_ref.shape[1]
    gn = b_ref.shape[2]
    hm = _rms(xm_ref[0], g).astype(MXU_DTYPE)
    z_ref[0] = _dot_nt(wz_ref[...], hm)
    dt_ref[0] = jax.nn.softplus(_dot_nt(wdt_ref[...], hm) + dtb_ref[...])
    xp = jnp.where(i > 0, xp_ref[0], 0.0)
    xn = jnp.where(i < nt - 1, xn_ref[0], 0.0)
    h_ext = _rms(jnp.concatenate([xp, xm_ref[0], xn], axis=0), g).astype(MXU_DTYPE)
    ext_ref[...] = _dot(h_ext, wx_ref[...])
    for c in range(ext_ref.shape[1] // LANES):
        cols = slice(c * LANES, (c + 1) * LANES)
        ext = ext_ref[:, cols]
        n_ext = ext.shape[0]
        acc = cb_ref[:, cols]
        for k in range(CONV_K):
            d = k - CONV_K // 2
            tap = ext if d == 0 else pltpu.roll(ext, (n_ext - d) % n_ext, axis=0)
            acc = acc + cw_ref[k:k + 1, cols] * tap[HALO:HALO + tl]
        y = _silu(acc)
        lo = c * LANES
        if lo < d_inner:
            xc_ref[0, lo:lo + LANES, :] = y.T
        elif lo < d_inner + gn:
            b_ref[0, :, lo - d_inner:lo - d_inner + LANES] = y
        else:
            xc_ref[0, lo - gn:lo - gn + LANES, :] = y.T


def _ssd_chunks(xc_ref, b_ref, dt_ref, alog_ref, y_ref, st_ref, *, rev, first):
    q = LANES
    tl = dt_ref.shape[2]
    nch = tl // q
    hp = HEADS_PER_GROUP * SSM_HEADDIM
    d_inner = SSM_HEADS * SSM_HEADDIM

    @pl.when(first)
    def _():
        st_ref[...] = jnp.zeros(st_ref.shape, jnp.float32)

    row = lax.broadcasted_iota(jnp.int32, (q, q), 0)
    col = lax.broadcasted_iota(jnp.int32, (q, q), 1)
    mask = (row >= col) if rev else (row <= col)
    u_sl = mask.astype(jnp.float32)
    u_ls = ((col >= row) if rev else (col <= row)).astype(jnp.float32)
    ones = jnp.ones((q, LANES), jnp.float32)
    a = -jnp.exp(alog_ref[...]) * LOG2E

    order = range(nch - 1, -1, -1) if rev else range(nch)
    for ci in order:
        lanes = slice(ci * q, (ci + 1) * q)
        dt = dt_ref[0, :, lanes]
        la = dt * a
        cs = _dot_f32(la, u_sl)
        cs_t = _dot_f32(u_ls, la, _NT)
        tot = _dot_f32(la, ones)
        ecs = jnp.exp2(cs)
        dte = jnp.exp2(tot - cs)
        etot = jnp.exp2(tot)
        for g in range(SSM_GROUPS):
            gcols = slice(g * D_STATE, (g + 1) * D_STATE)
            bm = b_ref[0, lanes, gcols].astype(MXU_DTYPE)
            ct = xc_ref[0, d_inner + g * D_STATE:d_inner + (g + 1) * D_STATE, lanes]
            ct = ct.astype(MXU_DTYPE)
            bct = _dot(bm, ct)
            r0 = g * hp
            prev = st_ref[r0:r0 + hp, :]
            y_off = _dot(prev.astype(MXU_DTYPE), ct)
            xdte = []
            for r in range(HEADS_PER_GROUP):
                hd = g * HEADS_PER_GROUP + r
                rows = slice(r0 + r * SSM_HEADDIM, r0 + (r + 1) * SSM_HEADDIM)
                xdt = xc_ref[0, rows, lanes] * dt[hd:hd + 1, :]
                seg = cs[hd:hd + 1, :] - cs_t[:, hd:hd + 1]
                mt = bct * jnp.exp2(jnp.where(mask, seg, NEG_BIG))
                y_diag = _dot(xdt.astype(MXU_DTYPE), mt.astype(MXU_DTYPE))
                y_ref[rows, lanes] = (y_diag + y_off[r * SSM_HEADDIM:(r + 1) * SSM_HEADDIM]
                                      * ecs[hd:hd + 1, :])
                xdte.append(xdt * dte[hd:hd + 1, :])
                st_ref[rows, :] = prev[r * SSM_HEADDIM:(r + 1) * SSM_HEADDIM] * etot[hd:hd + 1, :]
            states = _dot(jnp.concatenate(xdte, axis=0).astype(MXU_DTYPE), bm)
            st_ref[r0:r0 + hp, :] = st_ref[r0:r0 + hp, :] + states


def _ssd_fwd_kernel(xc_ref, b_ref, dt_ref, alog_ref, y_ref, st_ref):
    _ssd_chunks(xc_ref, b_ref, dt_ref, alog_ref, y_ref.at[0], st_ref,
                rev=False, first=pl.program_id(1) == 0)


def _ssd_bwd_kernel(xc_ref, b_ref, dt_ref, alog_ref, yf_ref, z_ref, dcol_ref, ng_ref,
                    wout_ref, x_ref, o_ref, st_ref, yb_ref):
    _ssd_chunks(xc_ref, b_ref, dt_ref, alog_ref, yb_ref, st_ref,
                rev=True, first=pl.program_id(1) == 0)
    d_inner = yf_ref.shape[1]
    y = yf_ref[0] + yb_ref[...] + xc_ref[0, :d_inner, :] * dcol_ref[...]
    y = y * _silu(z_ref[0])
    gsz = d_inner // SSM_GROUPS
    parts = []
    for g in range(SSM_GROUPS):
        yg = y[g * gsz:(g + 1) * gsz]
        parts.append(yg * lax.rsqrt(jnp.mean(yg * yg, axis=0, keepdims=True) + EPS))
    y = (jnp.concatenate(parts, axis=0) * ng_ref[...]).astype(MXU_DTYPE)
    out_t = _dot(wout_ref[...], y)
    o_ref[0] = x_ref[0] + out_t.T


def _prep_ssm(w_in, conv_w, conv_b, dt_bias, a_log, d_skip, norm_g, w_out, mix_norm):
    d = w_in.shape[0]
    d_inner = SSM_HEADS * SSM_HEADDIM
    conv_dim = d_inner + 2 * SSM_GROUPS * D_STATE
    w = w_in.astype(MXU_DTYPE)
    return dict(
        g=mix_norm.reshape(1, d), wz=w[:, :d_inner].T, wx=w[:, d_inner:d_inner + conv_dim],
        wdt=w[:, d_inner + conv_dim:].T, dtb=dt_bias.reshape(2 * SSM_HEADS, 1),
        conv_w=conv_w, conv_b=conv_b.reshape(1, conv_dim),
        alog=a_log.reshape(2, SSM_HEADS, 1),
        dcol=jnp.repeat(d_skip, SSM_HEADDIM).reshape(d_inner, 1),
        ng=norm_g.reshape(d_inner, 1), wout=w_out.T.astype(MXU_DTYPE))


def _mamba(x, p):
    b, l, d = x.shape
    d_inner = SSM_HEADS * SSM_HEADDIM
    gn = SSM_GROUPS * D_STATE
    conv_dim = d_inner + 2 * gn
    nh2 = 2 * SSM_HEADS

    tl = _tile(l, 256)
    nt = l // tl
    per = tl // HALO
    last_halo = l // HALO - 1
    z_t, xc_t, b_m, dt_t = pl.pallas_call(
        functools.partial(_ssm_in_kernel, nt=nt),
        grid=(b, nt),
        in_specs=[
            pl.BlockSpec((1, tl, d), lambda bi, i: (bi, i, 0)),
            pl.BlockSpec((1, HALO, d), lambda bi, i: (bi, jnp.maximum(i * per - 1, 0), 0)),
            pl.BlockSpec((1, HALO, d),
                         lambda bi, i: (bi, jnp.minimum((i + 1) * per, last_halo), 0)),
            _resident((1, d)), _resident((d_inner, d)), _resident((d, conv_dim)),
            _resident((nh2, d)), _resident((nh2, 1)), _resident((CONV_K, conv_dim)),
            _resident((1, conv_dim)),
        ],
        out_specs=[
            pl.BlockSpec((1, d_inner, tl), lambda bi, i: (bi, 0, i)),
            pl.BlockSpec((1, d_inner + gn, tl), lambda bi, i: (bi, 0, i)),
            pl.BlockSpec((1, tl, gn), lambda bi, i: (bi, i, 0)),
            pl.BlockSpec((1, nh2, tl), lambda bi, i: (bi, 0, i)),
        ],
        out_shape=[
            jax.ShapeDtypeStruct((b, d_inner, l), jnp.float32),
            jax.ShapeDtypeStruct((b, d_inner + gn, l), jnp.float32),
            jax.ShapeDtypeStruct((b, l, gn), jnp.float32),
            jax.ShapeDtypeStruct((b, nh2, l), jnp.float32),
        ],
        scratch_shapes=[pltpu.VMEM((tl + 2 * HALO, conv_dim), jnp.float32)],
        compiler_params=_params("parallel", "parallel"),
        name="ssm_in",
    )(x, x, x, p["g"], p["wz"], p["wx"], p["wdt"], p["dtb"], p["conv_w"], p["conv_b"])

    ts = _tile(l, 256)
    nb = l // ts
    st_shape = pltpu.VMEM((d_inner, D_STATE), jnp.float32)

    def scan_specs(tmap):
        return [
            pl.BlockSpec((1, d_inner + gn, ts), lambda bi, i: (bi, 0, tmap(i))),
            pl.BlockSpec((1, ts, gn), lambda bi, i: (bi, tmap(i), 0)),
        ]

    y_f = pl.pallas_call(
        _ssd_fwd_kernel,
        grid=(b, nb),
        in_specs=scan_specs(lambda i: i) + [
            pl.BlockSpec((1, SSM_HEADS, ts), lambda bi, i: (bi, 0, i)),
            pl.BlockSpec((None, SSM_HEADS, 1), lambda bi, i: (0, 0, 0)),
        ],
        out_specs=pl.BlockSpec((1, d_inner, ts), lambda bi, i: (bi, 0, i)),
        out_shape=jax.ShapeDtypeStruct((b, d_inner, l), jnp.float32),
        scratch_shapes=[st_shape],
        compiler_params=_params("parallel", "arbitrary"),
        name="ssd_fwd",
    )(xc_t, b_m, dt_t, p["alog"])

    rmap = lambda i: nb - 1 - i
    return pl.pallas_call(
        _ssd_bwd_kernel,
        grid=(b, nb),
        in_specs=scan_specs(rmap) + [
            pl.BlockSpec((1, SSM_HEADS, ts), lambda bi, i: (bi, 1, rmap(i))),
            pl.BlockSpec((None, SSM_HEADS, 1), lambda bi, i: (1, 0, 0)),
            pl.BlockSpec((1, d_inner, ts), lambda bi, i: (bi, 0, rmap(i))),
            pl.BlockSpec((1, d_inner, ts), lambda bi, i: (bi, 0, rmap(i))),
            _resident((d_inner, 1)), _resident((d_inner, 1)), _resident((d, d_inner)),
            pl.BlockSpec((1, ts, d), lambda bi, i: (bi, rmap(i), 0)),
        ],
        out_specs=pl.BlockSpec((1, ts, d), lambda bi, i: (bi, rmap(i), 0)),
        out_shape=jax.ShapeDtypeStruct((b, l, d), jnp.float32),
        scratch_shapes=[st_shape, pltpu.VMEM((d_inner, ts), jnp.float32)],
        compiler_params=_params("parallel", "arbitrary"),
        name="ssd_bwd",
    )(xc_t, b_m, dt_t, p["alog"], y_f, z_t, p["dcol"], p["ng"], p["wout"], x)


def kernel(x_prompt, x_sample, ffn1_norm, ffn1_w_in, ffn1_w_out, mix_norm, ffn2_norm, ffn2_w_in, ffn2_w_out, mla_w_a, mla_q_norm, mla_kv_norm, mla_w_uq, mla_w_uk, mla_w_uv, mla_w_o, ssm_w_in, ssm_conv_w, ssm_conv_b, ssm_dt_bias, ssm_a_log, ssm_d, ssm_norm, ssm_w_out, final_norm):
    depth = ffn1_norm.shape[0]
    n_mixers = 2
    c = lambda w: w.astype(MXU_DTYPE)
    w1_in, w1_out, w2_in, w2_out = c(ffn1_w_in), c(ffn1_w_out), c(ffn2_w_in), c(ffn2_w_out)
    mixers = []
    for i in range(depth):
        j = i // n_mixers
        if i % n_mixers == 0:
            mixers.append(_prep_mla(mla_w_a[j], mla_q_norm[j], mla_kv_norm[j], mla_w_uq[j],
                                    mla_w_uk[j], mla_w_uv[j], mla_w_o[j], mix_norm[i]))
        else:
            mixers.append(_prep_ssm(ssm_w_in[j], ssm_conv_w[j], ssm_conv_b[j], ssm_dt_bias[j],
                                    ssm_a_log[j], ssm_d[j], ssm_norm[j], ssm_w_out[j],
                                    mix_norm[i]))

    def trunk(x):
        b, s, d = x.shape
        for i in range(depth):
            x = _ffn(x.reshape(b * s, d), ffn1_norm[i], w1_in[i], w1_out[i]).reshape(b, s, d)
            x = _mla(x, mixers[i]) if i % n_mixers == 0 else _mamba(x, mixers[i])
            fg = final_norm if i == depth - 1 else None
            x = _ffn(x.reshape(b * s, d), ffn2_norm[i], w2_in[i], w2_out[i], fg).reshape(b, s, d)
        return x

    return (trunk(x_prompt), trunk(x_sample))
```

```python
import functools

import jax
import jax.numpy as jnp
from jax import lax
from jax.experimental import pallas as pl
from jax.experimental.pallas import tpu as pltpu

EPS = 1e-6
MLA_HEADS = 8
QK_NOPE = 128
QK_ROPE = 64
Q_RANK = 256
KV_RANK = 256
ROPE_THETA = 10000.0
SOFTMAX_SCALE = (QK_NOPE + QK_ROPE) ** -0.5
LOG2E = 1.4426950408889634
QK_SCALE = SOFTMAX_SCALE * LOG2E
SSM_HEADDIM = 64
SSM_HEADS = 32
SSM_GROUPS = 4
HEADS_PER_GROUP = SSM_HEADS // SSM_GROUPS
D_STATE = 128
CONV_K = 5

LANES = 128
HALO = 8
VMEM_LIMIT_BYTES = 56 * 1024 * 1024
MXU_TILE = 256

FFN_TOKENS = 1024
FFN_CHUNK = MXU_TILE
MLA_PROJ_TOKENS = 512
ATTN_QUERIES = 128
ATTN_KEYS = 2048
ATTN_KEY_TILES = 4
SSM_IN_TOKENS = 512
SSD_TOKENS = 256

MXU_DTYPE = jnp.bfloat16
ROPE_PAD = LANES
QK_WIDTH = KV_RANK + ROPE_PAD
NEG_BIG = -1e30

_NT = (((1,), (1,)), ((), ()))


def _dot(a, b):
    return jnp.dot(a, b, preferred_element_type=jnp.float32)


def _dot_nt(a, b):
    return lax.dot_general(a, b, _NT, preferred_element_type=jnp.float32)


def _dot_f32(a, b, dims=None):
    if dims is None:
        return jnp.dot(a, b, preferred_element_type=jnp.float32,
                       precision=lax.Precision.HIGHEST)
    return lax.dot_general(a, b, dims, preferred_element_type=jnp.float32,
                           precision=lax.Precision.HIGHEST)


def _rms(x, g):
    return x * lax.rsqrt(jnp.mean(x * x, axis=-1, keepdims=True) + EPS) * g


def _silu(x):
    return x * jax.nn.sigmoid(x)


def _params(*sem):
    return pltpu.CompilerParams(dimension_semantics=sem, vmem_limit_bytes=VMEM_LIMIT_BYTES)


def _tile(n, want):
    t = min(n, want)
    assert n % t == 0, (n, t)
    return t


def _resident(shape):
    nd = len(shape)
    return pl.BlockSpec(shape, lambda *_: (0,) * nd, pipeline_mode=pl.Buffered(1))


def _ffn_kernel(*refs, n_chunks, chunk, final):
    if final:
        x_ref, g_ref, wg_ref, wu_ref, wo_ref, fg_ref, o_ref, a_ref = refs
    else:
        x_ref, g_ref, wg_ref, wu_ref, wo_ref, o_ref, a_ref = refs
    x = x_ref[...]
    h = _rms(x, g_ref[...]).astype(MXU_DTYPE)
    for c in range(n_chunks):
        sl = slice(c * chunk, (c + 1) * chunk)
        gate = _dot(h, wg_ref[:, sl])
        up = _dot(h, wu_ref[:, sl])
        a_ref[:, sl] = (_silu(gate) * up).astype(MXU_DTYPE)
    y = x + 0.5 * _dot(a_ref[...], wo_ref[...])
    if final:
        y = _rms(y, fg_ref[...])
    o_ref[...] = y


def _ffn(x, norm_g, w_in, w_out, final_g=None):
    t, d = x.shape
    f = w_out.shape[0]
    tm = _tile(t, FFN_TOKENS)
    chunk = FFN_CHUNK if f % FFN_CHUNK == 0 else f
    final = final_g is not None
    in_specs = [
        pl.BlockSpec((tm, d), lambda i: (i, 0)),
        _resident((1, d)),
        pl.BlockSpec((d, f), lambda i: (0, 0), pipeline_mode=pl.Buffered(1)),
        pl.BlockSpec((d, f), lambda i: (0, 1), pipeline_mode=pl.Buffered(1)),
        _resident((f, d)),
    ]
    args = [x, norm_g.reshape(1, d), w_in, w_in, w_out]
    if final:
        in_specs.append(_resident((1, d)))
        args.append(final_g.reshape(1, d))
    return pl.pallas_call(
        functools.partial(_ffn_kernel, n_chunks=f // chunk, chunk=chunk, final=final),
        grid=(t // tm,),
        in_specs=in_specs,
        out_specs=pl.BlockSpec((tm, d), lambda i: (i, 0)),
        out_shape=jax.ShapeDtypeStruct((t, d), jnp.float32),
        scratch_shapes=[pltpu.VMEM((tm, f), MXU_DTYPE)],
        compiler_params=_params("parallel"),
        name="ffn_final" if final else "ffn",
    )(*args)


def _mla_proj_kernel(x_ref, g_ref, wa_ref, qn_ref, kvn_ref, wqn_ref, wqa_ref, wqb_ref,
                     wukt_ref, cos_ref, sin_ref, q_ref, k_ref):
    h = _rms(x_ref[0], g_ref[...]).astype(MXU_DTYPE)
    c = _dot(h, wa_ref[...])
    c_q = _rms(c[:, :Q_RANK], qn_ref[...]).astype(MXU_DTYPE)
    c_kv = _rms(c[:, Q_RANK:Q_RANK + KV_RANK], kvn_ref[...])
    cos = cos_ref[...]
    sin = sin_ref[...]
    k0 = Q_RANK + KV_RANK
    k_rope = c[:, k0:k0 + ROPE_PAD] * cos + c[:, k0 + ROPE_PAD:k0 + 2 * ROPE_PAD] * sin
    k_ref[0, :, :KV_RANK] = c_kv.astype(MXU_DTYPE)
    k_ref[0, :, KV_RANK:] = k_rope.astype(MXU_DTYPE)
    q_nope = _dot(c_q, wqn_ref[...]).astype(MXU_DTYPE)
    q_a = _dot(c_q, wqa_ref[...])
    q_b = _dot(c_q, wqb_ref[...])
    for hd in range(MLA_HEADS):
        q_lat = _dot(q_nope[:, hd * QK_NOPE:(hd + 1) * QK_NOPE], wukt_ref[hd])
        sl = slice(hd * ROPE_PAD, (hd + 1) * ROPE_PAD)
        q_rope = q_a[:, sl] * cos + q_b[:, sl] * sin
        q_ref[0, hd, :, :KV_RANK] = (q_lat * QK_SCALE).astype(MXU_DTYPE)
        q_ref[0, hd, :, KV_RANK:] = (q_rope * QK_SCALE).astype(MXU_DTYPE)


def _mla_attn_kernel(q_ref, k_ref, x_ref, wuv_ref, wo_ref, o_ref, m_ref, l_ref, acc_ref,
                     s_ref, *, tk, nk):
    nh, tq = q_ref.shape[1], q_ref.shape[2]
    q = q_ref[0].reshape(nh * tq, QK_WIDTH)
    m_ref[...] = jnp.full(m_ref.shape, -jnp.inf, jnp.float32)
    l_ref[...] = jnp.zeros(l_ref.shape, jnp.float32)
    acc_ref[...] = jnp.zeros(acc_ref.shape, jnp.float32)
    ntile = tk // LANES

    def scores(kt):
        return _dot_nt(q, k_ref[0, pl.ds(pl.multiple_of(kt * tk, tk), tk), :])

    def consume(slot, kt):
        s = s_ref[slot]
        tiles = [s[:, j * LANES:(j + 1) * LANES] for j in range(ntile)]
        m_prev = m_ref[...]
        m_new = jnp.maximum(m_prev, jnp.max(functools.reduce(jnp.maximum, tiles),
                                            axis=1, keepdims=True))
        p = [jnp.exp2(t - m_new) for t in tiles]
        alpha = jnp.exp2(m_prev - m_new)
        l_ref[...] = alpha * l_ref[...] + functools.reduce(jnp.add, p)
        m_ref[...] = m_new
        v = k_ref[0, pl.ds(pl.multiple_of(kt * tk, tk), tk), :KV_RANK]
        pv = _dot(jnp.concatenate(p, axis=1).astype(MXU_DTYPE), v)
        acc_ref[...] = acc_ref[...] * jnp.concatenate([alpha] * (KV_RANK // LANES), axis=1) + pv

    s_ref[0] = scores(0)

    def pair(j, carry):
        t0 = 2 * j
        s_ref[1] = scores(t0 + 1)
        consume(0, t0)
        s_ref[0] = scores(t0 + 2)
        consume(1, t0 + 1)
        return carry

    lax.fori_loop(0, nk // 2 - 1, pair, 0)
    s_ref[1] = scores(nk - 1)
    consume(0, nk - 2)
    consume(1, nk - 1)
    l = jnp.sum(l_ref[...], axis=1, keepdims=True)
    o_lat = (acc_ref[...] / l).astype(MXU_DTYPE)
    heads = [_dot(o_lat[hd * tq:(hd + 1) * tq], wuv_ref[hd]) for hd in range(nh)]
    o = jnp.concatenate(heads, axis=1).astype(MXU_DTYPE)
    o_ref[0] = x_ref[0] + _dot(o, wo_ref[...])


def _rope_tables(seq):
    inv = ROPE_THETA ** (-jnp.arange(0, QK_ROPE, 2, dtype=jnp.float32) / QK_ROPE)
    ang = jnp.arange(seq, dtype=jnp.float32)[:, None] * inv[None, :]
    pad = jnp.zeros((seq, ROPE_PAD - QK_ROPE), jnp.float32)
    cos, sin = jnp.cos(ang), jnp.sin(ang)
    return (jnp.concatenate([cos, cos, pad], axis=1), jnp.concatenate([sin, sin, pad], axis=1))


def _prep_mla(w_a, q_norm, kv_norm, w_uq, w_uk, w_uv, w_o, mix_norm):
    d = w_a.shape[0]
    half = QK_ROPE // 2
    k0 = Q_RANK + KV_RANK
    zpad = jnp.zeros((d, ROPE_PAD - QK_ROPE), w_a.dtype)
    wa = jnp.concatenate([w_a, zpad, -w_a[:, k0 + half:], w_a[:, k0:k0 + half], zpad], axis=1)
    r1 = w_uq[:, :, QK_NOPE:QK_NOPE + half]
    r2 = w_uq[:, :, QK_NOPE + half:]
    zq = jnp.zeros((Q_RANK, MLA_HEADS, ROPE_PAD - QK_ROPE), w_uq.dtype)
    wqa = jnp.concatenate([r1, r2, zq], axis=2).reshape(Q_RANK, MLA_HEADS * ROPE_PAD)
    wqb = jnp.concatenate([-r2, r1, zq], axis=2).reshape(Q_RANK, MLA_HEADS * ROPE_PAD)
    wqn = w_uq[:, :, :QK_NOPE].reshape(Q_RANK, MLA_HEADS * QK_NOPE)
    c = lambda w: w.astype(MXU_DTYPE)
    return dict(
        g=mix_norm.reshape(1, d), wa=c(wa), qn=q_norm.reshape(1, Q_RANK),
        kvn=kv_norm.reshape(1, KV_RANK), wqn=c(wqn), wqa=c(wqa), wqb=c(wqb),
        wukt=c(jnp.transpose(w_uk, (1, 2, 0))), wuv=c(jnp.transpose(w_uv, (1, 0, 2))),
        wo=c(w_o))


def _mla(x, p):
    b, s, d = x.shape
    cos, sin = _rope_tables(s)
    tm = _tile(s, MLA_PROJ_TOKENS)
    q, k = pl.pallas_call(
        _mla_proj_kernel,
        grid=(b, s // tm),
        in_specs=[
            pl.BlockSpec((1, tm, d), lambda bi, i: (bi, i, 0)),
            _resident((1, d)), _resident(p["wa"].shape), _resident((1, Q_RANK)),
            _resident((1, KV_RANK)), _resident(p["wqn"].shape), _resident(p["wqa"].shape),
            _resident(p["wqb"].shape), _resident(p["wukt"].shape),
            pl.BlockSpec((tm, ROPE_PAD), lambda bi, i: (i, 0)),
            pl.BlockSpec((tm, ROPE_PAD), lambda bi, i: (i, 0)),
        ],
        out_specs=[
            pl.BlockSpec((1, MLA_HEADS, tm, QK_WIDTH), lambda bi, i: (bi, 0, i, 0)),
            pl.BlockSpec((1, tm, QK_WIDTH), lambda bi, i: (bi, i, 0)),
        ],
        out_shape=[
            jax.ShapeDtypeStruct((b, MLA_HEADS, s, QK_WIDTH), MXU_DTYPE),
            jax.ShapeDtypeStruct((b, s, QK_WIDTH), MXU_DTYPE),
        ],
        compiler_params=_params("parallel", "parallel"),
        name="mla_proj",
    )(x, p["g"], p["wa"], p["qn"], p["kvn"], p["wqn"], p["wqa"], p["wqb"], p["wukt"], cos, sin)

    tq = _tile(s, ATTN_QUERIES)
    tk = min(ATTN_KEYS, max(LANES, s // ATTN_KEY_TILES))
    assert s % (2 * tk) == 0, (s, tk)
    rows = MLA_HEADS * tq
    return pl.pallas_call(
        functools.partial(_mla_attn_kernel, tk=tk, nk=s // tk),
        grid=(b, s // tq),
        in_specs=[
            pl.BlockSpec((1, MLA_HEADS, tq, QK_WIDTH), lambda bi, i: (bi, 0, i, 0)),
            pl.BlockSpec((1, s, QK_WIDTH), lambda bi, i: (bi, 0, 0)),
            pl.BlockSpec((1, tq, d), lambda bi, i: (bi, i, 0)),
            _resident(p["wuv"].shape), _resident(p["wo"].shape),
        ],
        out_specs=pl.BlockSpec((1, tq, d), lambda bi, i: (bi, i, 0)),
        out_shape=jax.ShapeDtypeStruct((b, s, d), jnp.float32),
        scratch_shapes=[pltpu.VMEM((rows, LANES), jnp.float32),
                        pltpu.VMEM((rows, LANES), jnp.float32),
                        pltpu.VMEM((rows, KV_RANK), jnp.float32),
                        pltpu.VMEM((2, rows, tk), jnp.float32)],
        compiler_params=_params("parallel", "arbitrary"),
        name="mla_attn",
    )(q, k, x, p["wuv"], p["wo"])


def _ssm_in_kernel(xm_ref, xp_ref, xn_ref, g_ref, wz_ref, wx_ref, wdt_ref, dtb_ref, cw_ref,
                   cb_ref, z_ref, xc_ref, b_ref, dt_ref, ext_ref, *, nt):
    i = pl.program_id(1)
    g = g_ref[...]
    tl = xm_ref.shape[1]
    d_inner = z_ref.shape[1]
    gn = b_ref.shape[2]
    hm = _rms(xm_ref[0], g).astype(MXU_DTYPE)
    z_ref[0] = _dot_nt(wz_ref[...], hm)
    dt_ref[0] = jax.nn.softplus(_dot_nt(wdt_ref[...], hm) + dtb_ref[...])
    xp = jnp.where(i > 0, xp_ref[0], 0.0)
    xn = jnp.where(i < nt - 1, xn_ref[0], 0.0)
    h_ext = _rms(jnp.concatenate([xp, xm_ref[0], xn], axis=0), g).astype(MXU_DTYPE)
    ext_ref[...] = _dot(h_ext, wx_ref[...])
    for c in range(ext_ref.shape[1] // LANES):
        cols = slice(c * LANES, (c + 1) * LANES)
        ext = ext_ref[:, cols]
        n_ext = ext.shape[0]
        acc = cb_ref[:, cols]
        for k in range(CONV_K):
            d = k - CONV_K // 2
            tap = ext if d == 0 else pltpu.roll(ext, (n_ext - d) % n_ext, axis=0)
            acc = acc + cw_ref[k:k + 1, cols] * tap[HALO:HALO + tl]
        y = _silu(acc)
        lo = c * LANES
        if lo < d_inner:
            xc_ref[0, lo:lo + LANES, :] = y.T
        elif lo < d_inner + gn:
            b_ref[0, :, lo - d_inner:lo - d_inner + LANES] = y
        else:
            xc_ref[0, lo - gn:lo - gn + LANES, :] = y.T


def _ssd_chunks(xc_ref, b_ref, dt_ref, alog_ref, y_ref, st_ref, *, rev, first):
    q = LANES
    tl = dt_ref.shape[2]
    nch = tl // q
    hp = HEADS_PER_GROUP * SSM_HEADDIM
    d_inner = SSM_HEADS * SSM_HEADDIM

    @pl.when(first)
    def _():
        st_ref[...] = jnp.zeros(st_ref.shape, jnp.float32)

    row = lax.broadcasted_iota(jnp.int32, (q, q), 0)
    col = lax.broadcasted_iota(jnp.int32, (q, q), 1)
    mask = (row >= col) if rev else (row <= col)
    u_sl = mask.astype(jnp.float32)
    u_ls = ((col >= row) if rev else (col <= row)).astype(jnp.float32)
    ones = jnp.ones((q, LANES), jnp.float32)
    a = -jnp.exp(alog_ref[...]) * LOG2E

    order = range(nch - 1, -1, -1) if rev else range(nch)
    for ci in order:
        lanes = slice(ci * q, (ci + 1) * q)
        dt = dt_ref[0, :, lanes]
        la = dt * a
        cs = _dot_f32(la, u_sl)
        cs_t = _dot_f32(u_ls, la, _NT)
        tot = _dot_f32(la, ones)
        ecs = jnp.exp2(cs)
        dte = jnp.exp2(tot - cs)
        etot = jnp.exp2(tot)
        for g in range(SSM_GROUPS):
            gcols = slice(g * D_STATE, (g + 1) * D_STATE)
            bm = b_ref[0, lanes, gcols].astype(MXU_DTYPE)
            ct = xc_ref[0, d_inner + g * D_STATE:d_inner + (g + 1) * D_STATE, lanes]
            ct = ct.astype(MXU_DTYPE)
            bct = _dot(bm, ct)
            r0 = g * hp
            prev = st_ref[r0:r0 + hp, :]
            y_off = _dot(prev.astype(MXU_DTYPE), ct)
            xdte = []
            for r in range(HEADS_PER_GROUP):
                hd = g * HEADS_PER_GROUP + r
                rows = slice(r0 + r * SSM_HEADDIM, r0 + (r + 1) * SSM_HEADDIM)
                xdt = xc_ref[0, rows, lanes] * dt[hd:hd + 1, :]
                seg = cs[hd:hd + 1, :] - cs_t[:, hd:hd + 1]
                mt = bct * jnp.exp2(jnp.where(mask, seg, NEG_BIG))
                y_diag = _dot(xdt.astype(MXU_DTYPE), mt.astype(MXU_DTYPE))
                y_ref[rows, lanes] = (y_diag + y_off[r * SSM_HEADDIM:(r + 1) * SSM_HEADDIM]
                                      * ecs[hd:hd + 1, :])
                xdte.append(xdt * dte[hd:hd + 1, :])
                st_ref[rows, :] = prev[r * SSM_HEADDIM:(r + 1) * SSM_HEADDIM] * etot[hd:hd + 1, :]
            states = _dot(jnp.concatenate(xdte, axis=0).astype(MXU_DTYPE), bm)
            st_ref[r0:r0 + hp, :] = st_ref[r0:r0 + hp, :] + states


def _ssd_fwd_kernel(xc_ref, b_ref, dt_ref, alog_ref, y_ref, st_ref):
    _ssd_chunks(xc_ref, b_ref, dt_ref, alog_ref, y_ref.at[0], st_ref,
                rev=False, first=pl.program_id(1) == 0)


def _ssd_bwd_kernel(xc_ref, b_ref, dt_ref, alog_ref, yf_ref, z_ref, dcol_ref, ng_ref,
                    wout_ref, x_ref, o_ref, st_ref, yb_ref):
    _ssd_chunks(xc_ref, b_ref, dt_ref, alog_ref, yb_ref, st_ref,
                rev=True, first=pl.program_id(1) == 0)
    d_inner = yf_ref.shape[1]
    y = yf_ref[0] + yb_ref[...] + xc_ref[0, :d_inner, :] * dcol_ref[...]
    y = y * _silu(z_ref[0])
    gsz = d_inner // SSM_GROUPS
    parts = []
    for g in range(SSM_GROUPS):
        yg = y[g * gsz:(g + 1) * gsz]
        parts.append(yg * lax.rsqrt(jnp.mean(yg * yg, axis=0, keepdims=True) + EPS))
    y = (jnp.concatenate(parts, axis=0) * ng_ref[...]).astype(MXU_DTYPE)
    out_t = _dot(wout_ref[...], y)
    o_ref[0] = x_ref[0] + out_t.T


def _prep_ssm(w_in, conv_w, conv_b, dt_bias, a_log, d_skip, norm_g, w_out, mix_norm):
    d = w_in.shape[0]
    d_inner = SSM_HEADS * SSM_HEADDIM
    conv_dim = d_inner + 2 * SSM_GROUPS * D_STATE
    w = w_in.astype(MXU_DTYPE)
    return dict(
        g=mix_norm.reshape(1, d), wz=w[:, :d_inner].T, wx=w[:, d_inner:d_inner + conv_dim],
        wdt=w[:, d_inner + conv_dim:].T, dtb=dt_bias.reshape(2 * SSM_HEADS, 1),
        conv_w=conv_w, conv_b=conv_b.reshape(1, conv_dim),
        alog=a_log.reshape(2, SSM_HEADS, 1),
        dcol=jnp.repeat(d_skip, SSM_HEADDIM).reshape(d_inner, 1),
        ng=norm_g.reshape(d_inner, 1), wout=w_out.T.astype(MXU_DTYPE))


def _mamba(x, p):
    b, l, d = x.shape
    d_inner = SSM_HEADS * SSM_HEADDIM
    gn = SSM_GROUPS * D_STATE
    conv_dim = d_inner + 2 * gn
    nh2 = 2 * SSM_HEADS

    tl = _tile(l, SSM_IN_TOKENS)
    nt = l // tl
    per = tl // HALO
    last_halo = l // HALO - 1
    z_t, xc_t, b_m, dt_t = pl.pallas_call(
        functools.partial(_ssm_in_kernel, nt=nt),
        grid=(b, nt),
        in_specs=[
            pl.BlockSpec((1, tl, d), lambda bi, i: (bi, i, 0)),
            pl.BlockSpec((1, HALO, d), lambda bi, i: (bi, jnp.maximum(i * per - 1, 0), 0)),
            pl.BlockSpec((1, HALO, d),
                         lambda bi, i: (bi, jnp.minimum((i + 1) * per, last_halo), 0)),
            _resident((1, d)), _resident((d_inner, d)), _resident((d, conv_dim)),
            _resident((nh2, d)), _resident((nh2, 1)), _resident((CONV_K, conv_dim)),
            _resident((1, conv_dim)),
        ],
        out_specs=[
            pl.BlockSpec((1, d_inner, tl), lambda bi, i: (bi, 0, i)),
            pl.BlockSpec((1, d_inner + gn, tl), lambda bi, i: (bi, 0, i)),
            pl.BlockSpec((1, tl, gn), lambda bi, i: (bi, i, 0)),
            pl.BlockSpec((1, nh2, tl), lambda bi, i: (bi, 0, i)),
        ],
        out_shape=[
            jax.ShapeDtypeStruct((b, d_inner, l), jnp.float32),
            jax.ShapeDtypeStruct((b, d_inner + gn, l), jnp.float32),
            jax.ShapeDtypeStruct((b, l, gn), jnp.float32),
            jax.ShapeDtypeStruct((b, nh2, l), jnp.float32),
        ],
        scratch_shapes=[pltpu.VMEM((tl + 2 * HALO, conv_dim), jnp.float32)],
        compiler_params=_params("parallel", "parallel"),
        name="ssm_in",
    )(x, x, x, p["g"], p["wz"], p["wx"], p["wdt"], p["dtb"], p["conv_w"], p["conv_b"])

    ts = _tile(l, SSD_TOKENS)
    nb = l // ts
    st_shape = pltpu.VMEM((d_inner, D_STATE), jnp.float32)

    def scan_specs(tmap):
        return [
            pl.BlockSpec((1, d_inner + gn, ts), lambda bi, i: (bi, 0, tmap(i))),
            pl.BlockSpec((1, ts, gn), lambda bi, i: (bi, tmap(i), 0)),
        ]

    y_f = pl.pallas_call(
        _ssd_fwd_kernel,
        grid=(b, nb),
        in_specs=scan_specs(lambda i: i) + [
            pl.BlockSpec((1, SSM_HEADS, ts), lambda bi, i: (bi, 0, i)),
            pl.BlockSpec((None, SSM_HEADS, 1), lambda bi, i: (0, 0, 0)),
        ],
        out_specs=pl.BlockSpec((1, d_inner, ts), lambda bi, i: (bi, 0, i)),
        out_shape=jax.ShapeDtypeStruct((b, d_inner, l), jnp.float32),
        scratch_shapes=[st_shape],
        compiler_params=_params("parallel", "arbitrary"),
        name="ssd_fwd",
    )(xc_t, b_m, dt_t, p["alog"])

    rmap = lambda i: nb - 1 - i
    return pl.pallas_call(
        _ssd_bwd_kernel,
        grid=(b, nb),
        in_specs=scan_specs(rmap) + [
            pl.BlockSpec((1, SSM_HEADS, ts), lambda bi, i: (bi, 1, rmap(i))),
            pl.BlockSpec((None, SSM_HEADS, 1), lambda bi, i: (1, 0, 0)),
            pl.BlockSpec((1, d_inner, ts), lambda bi, i: (bi, 0, rmap(i))),
            pl.BlockSpec((1, d_inner, ts), lambda bi, i: (bi, 0, rmap(i))),
            _resident((d_inner, 1)), _resident((d_inner, 1)), _resident((d, d_inner)),
            pl.BlockSpec((1, ts, d), lambda bi, i: (bi, rmap(i), 0)),
        ],
        out_specs=pl.BlockSpec((1, ts, d), lambda bi, i: (bi, rmap(i), 0)),
        out_shape=jax.ShapeDtypeStruct((b, l, d), jnp.float32),
        scratch_shapes=[st_shape, pltpu.VMEM((d_inner, ts), jnp.float32)],
        compiler_params=_params("parallel", "arbitrary"),
        name="ssd_bwd",
    )(xc_t, b_m, dt_t, p["alog"], y_f, z_t, p["dcol"], p["ng"], p["wout"], x)


def kernel(x_prompt, x_sample, ffn1_norm, ffn1_w_in, ffn1_w_out, mix_norm, ffn2_norm, ffn2_w_in, ffn2_w_out, mla_w_a, mla_q_norm, mla_kv_norm, mla_w_uq, mla_w_uk, mla_w_uv, mla_w_o, ssm_w_in, ssm_conv_w, ssm_conv_b, ssm_dt_bias, ssm_a_log, ssm_d, ssm_norm, ssm_w_out, final_norm):
    depth = ffn1_norm.shape[0]
    n_mixers = 2
    c = lambda w: w.astype(MXU_DTYPE)
    w1_in, w1_out, w2_in, w2_out = c(ffn1_w_in), c(ffn1_w_out), c(ffn2_w_in), c(ffn2_w_out)
    mixers = []
    for i in range(depth):
        j = i // n_mixers
        if i % n_mixers == 0:
            mixers.append(_prep_mla(mla_w_a[j], mla_q_norm[j], mla_kv_norm[j], mla_w_uq[j],
                                    mla_w_uk[j], mla_w_uv[j], mla_w_o[j], mix_norm[i]))
        else:
            mixers.append(_prep_ssm(ssm_w_in[j], ssm_conv_w[j], ssm_conv_b[j], ssm_dt_bias[j],
                                    ssm_a_log[j], ssm_d[j], ssm_norm[j], ssm_w_out[j],
                                    mix_norm[i]))

    def trunk(x):
        b, s, d = x.shape
        for i in range(depth):
            x = _ffn(x.reshape(b * s, d), ffn1_norm[i], w1_in[i], w1_out[i]).reshape(b, s, d)
            x = _mla(x, mixers[i]) if i % n_mixers == 0 else _mamba(x, mixers[i])
            fg = final_norm if i == depth - 1 else None
            x = _ffn(x.reshape(b * s, d), ffn2_norm[i], w2_in[i], w2_out[i], fg).reshape(b, s, d)
        return x

    return (trunk(x_prompt), trunk(x_sample))
```

```python
import functools

import jax
import jax.numpy as jnp
from jax import lax
from jax.experimental import pallas as pl
from jax.experimental.pallas import tpu as pltpu

EPS = 1e-6
MLA_HEADS = 8
QK_NOPE = 128
QK_ROPE = 64
Q_RANK = 256
KV_RANK = 256
ROPE_THETA = 10000.0
SOFTMAX_SCALE = (QK_NOPE + QK_ROPE) ** -0.5
LOG2E = 1.4426950408889634
QK_SCALE = SOFTMAX_SCALE * LOG2E
SSM_HEADDIM = 64
SSM_HEADS = 32
SSM_GROUPS = 4
HEADS_PER_GROUP = SSM_HEADS // SSM_GROUPS
D_STATE = 128
CONV_K = 5

LANES = 128
HALO = 8
VMEM_LIMIT_BYTES = 56 * 1024 * 1024
MXU_TILE = 256

FFN_TOKENS = 1024
FFN_CHUNK = MXU_TILE
MLA_PROJ_TOKENS = 512
ATTN_QUERIES = 128
ATTN_KEYS = 1024
ATTN_KEY_TILES = 8
SSM_IN_TOKENS = 512
SSD_TOKENS = 256

MXU_DTYPE = jnp.bfloat16
ROPE_PAD = LANES
QK_WIDTH = KV_RANK + ROPE_PAD
NEG_BIG = -1e30

_NT = (((1,), (1,)), ((), ()))


def _dot(a, b):
    return jnp.dot(a, b, preferred_element_type=jnp.float32)


def _dot_nt(a, b):
    return lax.dot_general(a, b, _NT, preferred_element_type=jnp.float32)


def _dot_f32(a, b, dims=None):
    if dims is None:
        return jnp.dot(a, b, preferred_element_type=jnp.float32,
                       precision=lax.Precision.HIGHEST)
    return lax.dot_general(a, b, dims, preferred_element_type=jnp.float32,
                           precision=lax.Precision.HIGHEST)


def _rms(x, g):
    return x * lax.rsqrt(jnp.mean(x * x, axis=-1, keepdims=True) + EPS) * g


def _silu(x):
    return x * jax.nn.sigmoid(x)


def _params(*sem):
    return pltpu.CompilerParams(dimension_semantics=sem, vmem_limit_bytes=VMEM_LIMIT_BYTES)


def _tile(n, want):
    t = min(n, want)
    assert n % t == 0, (n, t)
    return t


def _resident(shape):
    nd = len(shape)
    return pl.BlockSpec(shape, lambda *_: (0,) * nd, pipeline_mode=pl.Buffered(1))


def _ffn_kernel(*refs, n_chunks, chunk, final):
    if final:
        x_ref, g_ref, wg_ref, wu_ref, wo_ref, fg_ref, o_ref, a_ref = refs
    else:
        x_ref, g_ref, wg_ref, wu_ref, wo_ref, o_ref, a_ref = refs
    x = x_ref[...]
    h = _rms(x, g_ref[...]).astype(MXU_DTYPE)
    for c in range(n_chunks):
        sl = slice(c * chunk, (c + 1) * chunk)
        gate = _dot(h, wg_ref[:, sl])
        up = _dot(h, wu_ref[:, sl])
        a_ref[:, sl] = (_silu(gate) * up).astype(MXU_DTYPE)
    y = x + 0.5 * _dot(a_ref[...], wo_ref[...])
    if final:
        y = _rms(y, fg_ref[...])
    o_ref[...] = y


def _ffn(x, norm_g, w_in, w_out, final_g=None):
    t, d = x.shape
    f = w_out.shape[0]
    tm = _tile(t, FFN_TOKENS)
    chunk = FFN_CHUNK if f % FFN_CHUNK == 0 else f
    final = final_g is not None
    in_specs = [
        pl.BlockSpec((tm, d), lambda i: (i, 0)),
        _resident((1, d)),
        pl.BlockSpec((d, f), lambda i: (0, 0), pipeline_mode=pl.Buffered(1)),
        pl.BlockSpec((d, f), lambda i: (0, 1), pipeline_mode=pl.Buffered(1)),
        _resident((f, d)),
    ]
    args = [x, norm_g.reshape(1, d), w_in, w_in, w_out]
    if final:
        in_specs.append(_resident((1, d)))
        args.append(final_g.reshape(1, d))
    return pl.pallas_call(
        functools.partial(_ffn_kernel, n_chunks=f // chunk, chunk=chunk, final=final),
        grid=(t // tm,),
        in_specs=in_specs,
        out_specs=pl.BlockSpec((tm, d), lambda i: (i, 0)),
        out_shape=jax.ShapeDtypeStruct((t, d), jnp.float32),
        scratch_shapes=[pltpu.VMEM((tm, f), MXU_DTYPE)],
        compiler_params=_params("parallel"),
        name="ffn_final" if final else "ffn",
    )(*args)


def _mla_proj_kernel(x_ref, g_ref, wa_ref, qn_ref, kvn_ref, wqn_ref, wqa_ref, wqb_ref,
                     wukt_ref, cos_ref, sin_ref, q_ref, k_ref):
    h = _rms(x_ref[0], g_ref[...]).astype(MXU_DTYPE)
    c = _dot(h, wa_ref[...])
    c_q = _rms(c[:, :Q_RANK], qn_ref[...]).astype(MXU_DTYPE)
    c_kv = _rms(c[:, Q_RANK:Q_RANK + KV_RANK], kvn_ref[...])
    cos = cos_ref[...]
    sin = sin_ref[...]
    k0 = Q_RANK + KV_RANK
    k_rope = c[:, k0:k0 + ROPE_PAD] * cos + c[:, k0 + ROPE_PAD:k0 + 2 * ROPE_PAD] * sin
    k_ref[0, :, :KV_RANK] = c_kv.astype(MXU_DTYPE)
    k_ref[0, :, KV_RANK:] = k_rope.astype(MXU_DTYPE)
    q_nope = _dot(c_q, wqn_ref[...]).astype(MXU_DTYPE)
    q_a = _dot(c_q, wqa_ref[...])
    q_b = _dot(c_q, wqb_ref[...])
    for hd in range(MLA_HEADS):
        q_lat = _dot(q_nope[:, hd * QK_NOPE:(hd + 1) * QK_NOPE], wukt_ref[hd])
        sl = slice(hd * ROPE_PAD, (hd + 1) * ROPE_PAD)
        q_rope = q_a[:, sl] * cos + q_b[:, sl] * sin
        q_ref[0, hd, :, :KV_RANK] = (q_lat * QK_SCALE).astype(MXU_DTYPE)
        q_ref[0, hd, :, KV_RANK:] = (q_rope * QK_SCALE).astype(MXU_DTYPE)


def _mla_attn_kernel(q_ref, k_ref, x_ref, wuv_ref, wo_ref, o_ref, m_ref, l_ref, acc_ref,
                     s_ref, *, tk, nk):
    nh, tq = q_ref.shape[1], q_ref.shape[2]
    q = q_ref[0].reshape(nh * tq, QK_WIDTH)
    m_ref[...] = jnp.full(m_ref.shape, -jnp.inf, jnp.float32)
    l_ref[...] = jnp.zeros(l_ref.shape, jnp.float32)
    acc_ref[...] = jnp.zeros(acc_ref.shape, jnp.float32)
    ntile = tk // LANES

    def scores(kt):
        return _dot_nt(q, k_ref[0, kt * tk:(kt + 1) * tk, :])

    def consume(slot, kt):
        s = s_ref[slot]
        tiles = [s[:, j * LANES:(j + 1) * LANES] for j in range(ntile)]
        m_prev = m_ref[...]
        m_new = jnp.maximum(m_prev, jnp.max(functools.reduce(jnp.maximum, tiles),
                                            axis=1, keepdims=True))
        p = [jnp.exp2(t - m_new) for t in tiles]
        alpha = jnp.exp2(m_prev - m_new)
        l_ref[...] = alpha * l_ref[...] + functools.reduce(jnp.add, p)
        m_ref[...] = m_new
        v = k_ref[0, kt * tk:(kt + 1) * tk, :KV_RANK]
        pv = _dot(jnp.concatenate(p, axis=1).astype(MXU_DTYPE), v)
        acc_ref[...] = acc_ref[...] * jnp.concatenate([alpha] * (KV_RANK // LANES), axis=1) + pv

    s_ref[0] = scores(0)
    for t in range(nk):
        if t + 1 < nk:
            s_ref[(t + 1) % 2] = scores(t + 1)
        consume(t % 2, t)
    l = jnp.sum(l_ref[...], axis=1, keepdims=True)
    o_lat = (acc_ref[...] / l).astype(MXU_DTYPE)
    heads = [_dot(o_lat[hd * tq:(hd + 1) * tq], wuv_ref[hd]) for hd in range(nh)]
    o = jnp.concatenate(heads, axis=1).astype(MXU_DTYPE)
    o_ref[0] = x_ref[0] + _dot(o, wo_ref[...])


def _rope_tables(seq):
    inv = ROPE_THETA ** (-jnp.arange(0, QK_ROPE, 2, dtype=jnp.float32) / QK_ROPE)
    ang = jnp.arange(seq, dtype=jnp.float32)[:, None] * inv[None, :]
    pad = jnp.zeros((seq, ROPE_PAD - QK_ROPE), jnp.float32)
    cos, sin = jnp.cos(ang), jnp.sin(ang)
    return (jnp.concatenate([cos, cos, pad], axis=1), jnp.concatenate([sin, sin, pad], axis=1))


def _prep_mla(w_a, q_norm, kv_norm, w_uq, w_uk, w_uv, w_o, mix_norm):
    d = w_a.shape[0]
    half = QK_ROPE // 2
    k0 = Q_RANK + KV_RANK
    zpad = jnp.zeros((d, ROPE_PAD - QK_ROPE), w_a.dtype)
    wa = jnp.concatenate([w_a, zpad, -w_a[:, k0 + half:], w_a[:, k0:k0 + half], zpad], axis=1)
    r1 = w_uq[:, :, QK_NOPE:QK_NOPE + half]
    r2 = w_uq[:, :, QK_NOPE + half:]
    zq = jnp.zeros((Q_RANK, MLA_HEADS, ROPE_PAD - QK_ROPE), w_uq.dtype)
    wqa = jnp.concatenate([r1, r2, zq], axis=2).reshape(Q_RANK, MLA_HEADS * ROPE_PAD)
    wqb = jnp.concatenate([-r2, r1, zq], axis=2).reshape(Q_RANK, MLA_HEADS * ROPE_PAD)
    wqn = w_uq[:, :, :QK_NOPE].reshape(Q_RANK, MLA_HEADS * QK_NOPE)
    c = lambda w: w.astype(MXU_DTYPE)
    return dict(
        g=mix_norm.reshape(1, d), wa=c(wa), qn=q_norm.reshape(1, Q_RANK),
        kvn=kv_norm.reshape(1, KV_RANK), wqn=c(wqn), wqa=c(wqa), wqb=c(wqb),
        wukt=c(jnp.transpose(w_uk, (1, 2, 0))), wuv=c(jnp.transpose(w_uv, (1, 0, 2))),
        wo=c(w_o))


def _mla(x, p):
    b, s, d = x.shape
    cos, sin = _rope_tables(s)
    tm = _tile(s, MLA_PROJ_TOKENS)
    q, k = pl.pallas_call(
        _mla_proj_kernel,
        grid=(b, s // tm),
        in_specs=[
            pl.BlockSpec((1, tm, d), lambda bi, i: (bi, i, 0)),
            _resident((1, d)), _resident(p["wa"].shape), _resident((1, Q_RANK)),
            _resident((1, KV_RANK)), _resident(p["wqn"].shape), _resident(p["wqa"].shape),
            _resident(p["wqb"].shape), _resident(p["wukt"].shape),
            pl.BlockSpec((tm, ROPE_PAD), lambda bi, i: (i, 0)),
            pl.BlockSpec((tm, ROPE_PAD), lambda bi, i: (i, 0)),
        ],
        out_specs=[
            pl.BlockSpec((1, MLA_HEADS, tm, QK_WIDTH), lambda bi, i: (bi, 0, i, 0)),
            pl.BlockSpec((1, tm, QK_WIDTH), lambda bi, i: (bi, i, 0)),
        ],
        out_shape=[
            jax.ShapeDtypeStruct((b, MLA_HEADS, s, QK_WIDTH), MXU_DTYPE),
            jax.ShapeDtypeStruct((b, s, QK_WIDTH), MXU_DTYPE),
        ],
        compiler_params=_params("parallel", "parallel"),
        name="mla_proj",
    )(x, p["g"], p["wa"], p["qn"], p["kvn"], p["wqn"], p["wqa"], p["wqb"], p["wukt"], cos, sin)

    tq = _tile(s, ATTN_QUERIES)
    tk = min(ATTN_KEYS, max(LANES, s // ATTN_KEY_TILES))
    assert s % tk == 0, (s, tk)
    rows = MLA_HEADS * tq
    return pl.pallas_call(
        functools.partial(_mla_attn_kernel, tk=tk, nk=s // tk),
        grid=(b, s // tq),
        in_specs=[
            pl.BlockSpec((1, MLA_HEADS, tq, QK_WIDTH), lambda bi, i: (bi, 0, i, 0)),
            pl.BlockSpec((1, s, QK_WIDTH), lambda bi, i: (bi, 0, 0)),
            pl.BlockSpec((1, tq, d), lambda bi, i: (bi, i, 0)),
            _resident(p["wuv"].shape), _resident(p["wo"].shape),
        ],
        out_specs=pl.BlockSpec((1, tq, d), lambda bi, i: (bi, i, 0)),
        out_shape=jax.ShapeDtypeStruct((b, s, d), jnp.float32),
        scratch_shapes=[pltpu.VMEM((rows, LANES), jnp.float32),
                        pltpu.VMEM((rows, LANES), jnp.float32),
                        pltpu.VMEM((rows, KV_RANK), jnp.float32),
                        pltpu.VMEM((2, rows, tk), jnp.float32)],
        compiler_params=_params("parallel", "arbitrary"),
        name="mla_attn",
    )(q, k, x, p["wuv"], p["wo"])


def _ssm_in_kernel(xm_ref, xp_ref, xn_ref, g_ref, wz_ref, wx_ref, wdt_ref, dtb_ref, cw_ref,
                   cb_ref, z_ref, xc_ref, b_ref, dt_ref, ext_ref, *, nt):
    i = pl.program_id(1)
    g = g_ref[...]
    tl = xm_ref.shape[1]
    d_inner = z_ref.shape[1]
    gn = b_ref.shape[2]
    hm = _rms(xm_ref[0], g).astype(MXU_DTYPE)
    z_ref[0] = _dot_nt(wz_ref[...], hm)
    dt_ref[0] = jax.nn.softplus(_dot_nt(wdt_ref[...], hm) + dtb_ref[...])
    xp = jnp.where(i > 0, xp_ref[0], 0.0)
    xn = jnp.where(i < nt - 1, xn_ref[0], 0.0)
    h_ext = _rms(jnp.concatenate([xp, xm_ref[0], xn], axis=0), g).astype(MXU_DTYPE)
    ext_ref[...] = _dot(h_ext, wx_ref[...])
    for c in range(ext_ref.shape[1] // LANES):
        cols = slice(c * LANES, (c + 1) * LANES)
        ext = ext_ref[:, cols]
        n_ext = ext.shape[0]
        acc = cb_ref[:, cols]
        for k in range(CONV_K):
            d = k - CONV_K // 2
            tap = ext if d == 0 else pltpu.roll(ext, (n_ext - d) % n_ext, axis=0)
            acc = acc + cw_ref[k:k + 1, cols] * tap[HALO:HALO + tl]
        y = _silu(acc)
        lo = c * LANES
        if lo < d_inner:
            xc_ref[0, lo:lo + LANES, :] = y.T
        elif lo < d_inner + gn:
            b_ref[0, :, lo - d_inner:lo - d_inner + LANES] = y
        else:
            xc_ref[0, lo - gn:lo - gn + LANES, :] = y.T


def _ssd_chunks(xc_ref, b_ref, dt_ref, alog_ref, y_ref, st_ref, *, rev, first):
    q = LANES
    tl = dt_ref.shape[2]
    nch = tl // q
    hp = HEADS_PER_GROUP * SSM_HEADDIM
    d_inner = SSM_HEADS * SSM_HEADDIM

    @pl.when(first)
    def _():
        st_ref[...] = jnp.zeros(st_ref.shape, jnp.float32)

    row = lax.broadcasted_iota(jnp.int32, (q, q), 0)
    col = lax.broadcasted_iota(jnp.int32, (q, q), 1)
    mask = (row >= col) if rev else (row <= col)
    u_sl = mask.astype(jnp.float32)
    u_ls = ((col >= row) if rev else (col <= row)).astype(jnp.float32)
    ones = jnp.ones((q, LANES), jnp.float32)
    a = -jnp.exp(alog_ref[...]) * LOG2E

    order = range(nch - 1, -1, -1) if rev else range(nch)
    for ci in order:
        lanes = slice(ci * q, (ci + 1) * q)
        dt = dt_ref[0, :, lanes]
        la = dt * a
        cs = _dot_f32(la, u_sl)
        cs_t = _dot_f32(u_ls, la, _NT)
        tot = _dot_f32(la, ones)
        ecs = jnp.exp2(cs)
        dte = jnp.exp2(tot - cs)
        etot = jnp.exp2(tot)
        for g in range(SSM_GROUPS):
            gcols = slice(g * D_STATE, (g + 1) * D_STATE)
            bm = b_ref[0, lanes, gcols].astype(MXU_DTYPE)
            ct = xc_ref[0, d_inner + g * D_STATE:d_inner + (g + 1) * D_STATE, lanes]
            ct = ct.astype(MXU_DTYPE)
            bct = _dot(bm, ct)
            r0 = g * hp
            prev = st_ref[r0:r0 + hp, :]
            y_off = _dot(prev.astype(MXU_DTYPE), ct)
            xdte = []
            for r in range(HEADS_PER_GROUP):
                hd = g * HEADS_PER_GROUP + r
                rows = slice(r0 + r * SSM_HEADDIM, r0 + (r + 1) * SSM_HEADDIM)
                xdt = xc_ref[0, rows, lanes] * dt[hd:hd + 1, :]
                seg = cs[hd:hd + 1, :] - cs_t[:, hd:hd + 1]
                mt = bct * jnp.exp2(jnp.where(mask, seg, NEG_BIG))
                y_diag = _dot(xdt.astype(MXU_DTYPE), mt.astype(MXU_DTYPE))
                y_ref[rows, lanes] = (y_diag + y_off[r * SSM_HEADDIM:(r + 1) * SSM_HEADDIM]
                                      * ecs[hd:hd + 1, :])
                xdte.append(xdt * dte[hd:hd + 1, :])
                st_ref[rows, :] = prev[r * SSM_HEADDIM:(r + 1) * SSM_HEADDIM] * etot[hd:hd + 1, :]
            states = _dot(jnp.concatenate(xdte, axis=0).astype(MXU_DTYPE), bm)
            st_ref[r0:r0 + hp, :] = st_ref[r0:r0 + hp, :] + states


def _ssd_fwd_kernel(xc_ref, b_ref, dt_ref, alog_ref, y_ref, st_ref):
    _ssd_chunks(xc_ref, b_ref, dt_ref, alog_ref, y_ref.at[0], st_ref,
                rev=False, first=pl.program_id(1) == 0)


def _ssd_bwd_kernel(xc_ref, b_ref, dt_ref, alog_ref, yf_ref, z_ref, dcol_ref, ng_ref,
                    wout_ref, x_ref, o_ref, st_ref, yb_ref):
    _ssd_chunks(xc_ref, b_ref, dt_ref, alog_ref, yb_ref, st_ref,
                rev=True, first=pl.program_id(1) == 0)
    d_inner = yf_ref.shape[1]
    reps = yf_ref.shape[2] // LANES
    dcol = jnp.concatenate([dcol_ref[...]] * reps, axis=1)
    y = yf_ref[0] + yb_ref[...] + xc_ref[0, :d_inner, :] * dcol
    y = y * _silu(z_ref[0])
    gsz = d_inner // SSM_GROUPS
    parts = []
    for g in range(SSM_GROUPS):
        yg = y[g * gsz:(g + 1) * gsz]
        parts.append(yg * lax.rsqrt(jnp.mean(yg * yg, axis=0, keepdims=True) + EPS))
    ng = jnp.concatenate([ng_ref[...]] * reps, axis=1)
    y = (jnp.concatenate(parts, axis=0) * ng).astype(MXU_DTYPE)
    out_t = _dot(wout_ref[...], y)
    o_ref[0] = x_ref[0] + out_t.T


def _prep_ssm(w_in, conv_w, conv_b, dt_bias, a_log, d_skip, norm_g, w_out, mix_norm):
    d = w_in.shape[0]
    d_inner = SSM_HEADS * SSM_HEADDIM
    conv_dim = d_inner + 2 * SSM_GROUPS * D_STATE
    w = w_in.astype(MXU_DTYPE)
    return dict(
        g=mix_norm.reshape(1, d), wz=w[:, :d_inner].T, wx=w[:, d_inner:d_inner + conv_dim],
        wdt=w[:, d_inner + conv_dim:].T, dtb=dt_bias.reshape(2 * SSM_HEADS, 1),
        conv_w=conv_w, conv_b=conv_b.reshape(1, conv_dim),
        alog=a_log.reshape(2, SSM_HEADS, 1),
        dcol=jnp.broadcast_to(jnp.repeat(d_skip, SSM_HEADDIM)[:, None], (d_inner, LANES)),
        ng=jnp.broadcast_to(norm_g[:, None], (d_inner, LANES)),
        wout=w_out.T.astype(MXU_DTYPE))


def _mamba(x, p):
    b, l, d = x.shape
    d_inner = SSM_HEADS * SSM_HEADDIM
    gn = SSM_GROUPS * D_STATE
    conv_dim = d_inner + 2 * gn
    nh2 = 2 * SSM_HEADS

    tl = _tile(l, SSM_IN_TOKENS)
    nt = l // tl
    per = tl // HALO
    last_halo = l // HALO - 1
    z_t, xc_t, b_m, dt_t = pl.pallas_call(
        functools.partial(_ssm_in_kernel, nt=nt),
        grid=(b, nt),
        in_specs=[
            pl.BlockSpec((1, tl, d), lambda bi, i: (bi, i, 0)),
            pl.BlockSpec((1, HALO, d), lambda bi, i: (bi, jnp.maximum(i * per - 1, 0), 0)),
            pl.BlockSpec((1, HALO, d),
                         lambda bi, i: (bi, jnp.minimum((i + 1) * per, last_halo), 0)),
            _resident((1, d)), _resident((d_inner, d)), _resident((d, conv_dim)),
            _resident((nh2, d)), _resident((nh2, 1)), _resident((CONV_K, conv_dim)),
            _resident((1, conv_dim)),
        ],
        out_specs=[
            pl.BlockSpec((1, d_inner, tl), lambda bi, i: (bi, 0, i)),
            pl.BlockSpec((1, d_inner + gn, tl), lambda bi, i: (bi, 0, i)),
            pl.BlockSpec((1, tl, gn), lambda bi, i: (bi, i, 0)),
            pl.BlockSpec((1, nh2, tl), lambda bi, i: (bi, 0, i)),
        ],
        out_shape=[
            jax.ShapeDtypeStruct((b, d_inner, l), jnp.float32),
            jax.ShapeDtypeStruct((b, d_inner + gn, l), jnp.float32),
            jax.ShapeDtypeStruct((b, l, gn), jnp.float32),
            jax.ShapeDtypeStruct((b, nh2, l), jnp.float32),
        ],
        scratch_shapes=[pltpu.VMEM((tl + 2 * HALO, conv_dim), jnp.float32)],
        compiler_params=_params("parallel", "parallel"),
        name="ssm_in",
    )(x, x, x, p["g"], p["wz"], p["wx"], p["wdt"], p["dtb"], p["conv_w"], p["conv_b"])

    ts = _tile(l, SSD_TOKENS)
    nb = l // ts
    st_shape = pltpu.VMEM((d_inner, D_STATE), jnp.float32)

    def scan_specs(tmap):
        return [
            pl.BlockSpec((1, d_inner + gn, ts), lambda bi, i: (bi, 0, tmap(i))),
            pl.BlockSpec((1, ts, gn), lambda bi, i: (bi, tmap(i), 0)),
        ]

    y_f = pl.pallas_call(
        _ssd_fwd_kernel,
        grid=(b, nb),
        in_specs=scan_specs(lambda i: i) + [
            pl.BlockSpec((1, SSM_HEADS, ts), lambda bi, i: (bi, 0, i)),
            pl.BlockSpec((None, SSM_HEADS, 1), lambda bi, i: (0, 0, 0)),
        ],
        out_specs=pl.BlockSpec((1, d_inner, ts), lambda bi, i: (bi, 0, i)),
        out_shape=jax.ShapeDtypeStruct((b, d_inner, l), jnp.float32),
        scratch_shapes=[st_shape],
        compiler_params=_params("parallel", "arbitrary"),
        name="ssd_fwd",
    )(xc_t, b_m, dt_t, p["alog"])

    rmap = lambda i: nb - 1 - i
    return pl.pallas_call(
        _ssd_bwd_kernel,
        grid=(b, nb),
        in_specs=scan_specs(rmap) + [
            pl.BlockSpec((1, SSM_HEADS, ts), lambda bi, i: (bi, 1, rmap(i))),
            pl.BlockSpec((None, SSM_HEADS, 1), lambda bi, i: (1, 0, 0)),
            pl.BlockSpec((1, d_inner, ts), lambda bi, i: (bi, 0, rmap(i))),
            pl.BlockSpec((1, d_inner, ts), lambda bi, i: (bi, 0, rmap(i))),
            _resident((d_inner, LANES)), _resident((d_inner, LANES)), _resident((d, d_inner)),
            pl.BlockSpec((1, ts, d), lambda bi, i: (bi, rmap(i), 0)),
        ],
        out_specs=pl.BlockSpec((1, ts, d), lambda bi, i: (bi, rmap(i), 0)),
        out_shape=jax.ShapeDtypeStruct((b, l, d), jnp.float32),
        scratch_shapes=[st_shape, pltpu.VMEM((d_inner, ts), jnp.float32)],
        compiler_params=_params("parallel", "arbitrary"),
        name="ssd_bwd",
    )(xc_t, b_m, dt_t, p["alog"], y_f, z_t, p["dcol"], p["ng"], p["wout"], x)


def kernel(x_prompt, x_sample, ffn1_norm, ffn1_w_in, ffn1_w_out, mix_norm, ffn2_norm, ffn2_w_in, ffn2_w_out, mla_w_a, mla_q_norm, mla_kv_norm, mla_w_uq, mla_w_uk, mla_w_uv, mla_w_o, ssm_w_in, ssm_conv_w, ssm_conv_b, ssm_dt_bias, ssm_a_log, ssm_d, ssm_norm, ssm_w_out, final_norm):
    depth = ffn1_norm.shape[0]
    n_mixers = 2
    c = lambda w: w.astype(MXU_DTYPE)
    w1_in, w1_out, w2_in, w2_out = c(ffn1_w_in), c(ffn1_w_out), c(ffn2_w_in), c(ffn2_w_out)
    mixers = []
    for i in range(depth):
        j = i // n_mixers
        if i % n_mixers == 0:
            mixers.append(_prep_mla(mla_w_a[j], mla_q_norm[j], mla_kv_norm[j], mla_w_uq[j],
                                    mla_w_uk[j], mla_w_uv[j], mla_w_o[j], mix_norm[i]))
        else:
            mixers.append(_prep_ssm(ssm_w_in[j], ssm_conv_w[j], ssm_conv_b[j], ssm_dt_bias[j],
                                    ssm_a_log[j], ssm_d[j], ssm_norm[j], ssm_w_out[j],
                                    mix_norm[i]))

    def trunk(x):
        b, s, d = x.shape
        for i in range(depth):
            x = _ffn(x.reshape(b * s, d), ffn1_norm[i], w1_in[i], w1_out[i]).reshape(b, s, d)
            x = _mla(x, mixers[i]) if i % n_mixers == 0 else _mamba(x, mixers[i])
            fg = final_norm if i == depth - 1 else None
            x = _ffn(x.reshape(b * s, d), ffn2_norm[i], w2_in[i], w2_out[i], fg).reshape(b, s, d)
        return x

    return (trunk(x_prompt), trunk(x_sample))
```
